```python
import jax
import jax.numpy as jnp
from jax import lax
import numpy as np

D_MODEL = 2048
BATCH = 2
SEQ = 4096
DEPTH = 4
DEC_BATCH = 128
DEC_SEQ = 4
PAST_LEN = 8192
PAGE_SIZE = 128

N_MIXERS = 4
HEAD_DIM = 128
N_HEADS = D_MODEL // HEAD_DIM
N_KV_HEADS = 2
KV_GROUP = N_HEADS // N_KV_HEADS
QKV_WIDTH = (N_HEADS + 2 * N_KV_HEADS) * HEAD_DIM
Q_BLOCK = 128
MLA_Q_RANK = D_MODEL // 4
MLA_KV_RANK = D_MODEL // 4
MLA_NOPE = 128
MLA_ROPE = 64
MLA_V = 128
MLA_SCALE = (MLA_NOPE + MLA_ROPE) ** -0.5
ROPE_THETA = 10000.0
FORGET_BIAS = 4.0
MOBA_BLOCK = 256
MOBA_TOPK = 3
N_EXPERTS = 16
N_GROUPS = 4
EXPERTS_PER_GROUP = N_EXPERTS // N_GROUPS
GROUP_SCORE_K = 2
TOP_K = 2
D_EXPERT = D_MODEL // 4
DN_ALPHA = (2 * DEPTH) ** 0.25
DN_BETA = (8 * DEPTH) ** -0.25
LN_EPS = 1e-5
RMS_EPS = 1e-6
N_SB_LAYERS = len(range(0, DEPTH, N_MIXERS))
N_MLA_LAYERS = len(range(1, DEPTH, N_MIXERS))
N_FOX_LAYERS = len(range(2, DEPTH, N_MIXERS))
N_MOBA_LAYERS = len(range(3, DEPTH, N_MIXERS))

kernel_name = 'hybrid_sb_mla_fox_moba_deepnorm_moe_step'

F32 = jnp.float32


def _einsum32(spec, a, b):
    return jnp.einsum(spec, a, b, preferred_element_type=F32)


def _layer_norm(x, g, b):
    xf = x.astype(F32)
    xc = xf - jnp.mean(xf, -1, keepdims=True)
    var = jnp.mean(xc * xc, -1, keepdims=True)
    return xc * lax.rsqrt(var + LN_EPS) * g.astype(F32) + b.astype(F32)


def _post_norm(x, h, g, b):
    return _layer_norm(DN_ALPHA * x.astype(F32) + h.astype(F32), g, b).astype(x.dtype)


def _rms_norm(x, g):
    xf = x.astype(F32)
    y = xf * lax.rsqrt(jnp.mean(xf * xf, -1, keepdims=True) + RMS_EPS) * g.astype(F32)
    return y.astype(x.dtype)


def _rope_angles(pos):
    half = MLA_ROPE // 2
    inv = ROPE_THETA ** (-jnp.arange(half, dtype=F32) / half)
    ang = pos.astype(F32)[:, None] * inv[None, :]
    return jnp.cos(ang), jnp.sin(ang)


def _rope(x, cos, sin):
    half = MLA_ROPE // 2
    x1 = x[..., :half].astype(F32)
    x2 = x[..., half:].astype(F32)
    return jnp.concatenate([x1 * cos - x2 * sin, x1 * sin + x2 * cos], -1).astype(x.dtype)


def _alibi_slopes():
    return jnp.exp2(-8.0 * jnp.arange(1, N_HEADS + 1, dtype=F32) / N_HEADS)


def _sb_core(q, q_pos, kv, k_pos):
    k, v = kv
    tq = q.shape[0]
    qg = q.reshape(tq, N_KV_HEADS, KV_GROUP, HEAD_DIM)
    z = _einsum32('tkgd,skd->kgts', qg, k) * HEAD_DIM ** -0.5
    past = k_pos[None, :] < q_pos[:, None]
    log_keep = jnp.where(past, jax.nn.log_sigmoid(-z), 0.0)
    suffix = lax.cumsum(log_keep, axis=3, reverse=True)
    between = jnp.concatenate([suffix[..., 1:], jnp.zeros_like(suffix[..., :1])], axis=3)
    w = jnp.where(past, jnp.exp(jax.nn.log_sigmoid(z) + between), 0.0)
    o = jnp.einsum('kgts,skd->tkgd', w.astype(v.dtype), v)
    return o.reshape(tq, N_HEADS, HEAD_DIM)


def _mla_core_expanded(q, q_pos, kv, k_pos):
    q_nope, q_pe = q
    k_nope, k_pe, v = kv
    s = (_einsum32('thd,shd->hts', q_nope, k_nope) + _einsum32('thr,sr->hts', q_pe, k_pe)) * MLA_SCALE
    s = jnp.where((k_pos[None, :] <= q_pos[:, None])[None], s, -jnp.inf)
    p = jax.nn.softmax(s, axis=-1)
    return jnp.einsum('hts,shd->thd', p.astype(v.dtype), v)


def _mla_core_latent(q, q_pos, kv, k_pos):
    q_lat, q_pe = q
    c_kv, k_pe = kv
    s = (_einsum32('thc,sc->hts', q_lat, c_kv) + _einsum32('thr,sr->hts', q_pe, k_pe)) * MLA_SCALE
    s = jnp.where((k_pos[None, :] <= q_pos[:, None])[None], s, -jnp.inf)
    p = jax.nn.softmax(s, axis=-1)
    return jnp.einsum('hts,sc->thc', p.astype(c_kv.dtype), c_kv)


def _fox_core(q, q_pos, kv, k_pos):
    k, v, logf = kv
    tq, tk = q.shape[0], k.shape[0]
    c = jnp.cumsum(logf.astype(F32), axis=0)
    cq = c[q_pos]
    qg = q.reshape(tq, N_KV_HEADS, KV_GROUP, HEAD_DIM)
    s = _einsum32('tkgd,skd->kgts', qg, k) * HEAD_DIM ** -0.5
    s = s + cq.T.reshape(N_KV_HEADS, KV_GROUP, tq, 1) - c.T.reshape(N_KV_HEADS, KV_GROUP, 1, tk)
    s = jnp.where(k_pos[None, :] <= q_pos[:, None], s, -jnp.inf)
    p = jax.nn.softmax(s, axis=-1)
    o = jnp.einsum('kgts,skd->tkgd', p.astype(v.dtype), v)
    return o.reshape(tq, N_HEADS, HEAD_DIM)


def _moba_core(q, q_pos, kv, k_pos):
    k, v = kv
    tq, tk = q.shape[0], k.shape[0]
    nb = -(-tk // MOBA_BLOCK)
    pad = nb * MOBA_BLOCK - tk
    kb = jnp.pad(k, ((0, pad), (0, 0), (0, 0))).reshape(nb, MOBA_BLOCK, N_KV_HEADS, HEAD_DIM)
    vb = jnp.pad(v, ((0, pad), (0, 0), (0, 0))).reshape(nb, MOBA_BLOCK, N_KV_HEADS, HEAD_DIM)
    kb_h = jnp.transpose(kb, (2, 0, 1, 3))
    vb_h = jnp.transpose(vb, (2, 0, 1, 3))
    kv_of_head = jnp.arange(N_HEADS) // KV_GROUP
    slopes = _alibi_slopes()
    scale = HEAD_DIM ** -0.5
    offs = jnp.arange(MOBA_BLOCK, dtype=jnp.int32)
    own = q_pos // MOBA_BLOCK
    qg = q.reshape(tq, N_KV_HEADS, KV_GROUP, HEAD_DIM)
    k_mean = jnp.mean(kb.astype(F32), axis=1)
    gate = _einsum32('tkgd,bkd->tkgb', qg.astype(F32), k_mean).reshape(tq, N_HEADS, nb)
    fully_past = jnp.arange(nb)[None, :] < own[:, None]
    gate = jnp.where(fully_past[:, None, :], gate, -jnp.inf)
    n_sel = min(MOBA_TOPK, nb)
    gate_top, sel = lax.top_k(gate, n_sel)
    keep = jnp.isfinite(gate_top)

    def alibi(key_pos):
        return -slopes[None, :, None] * (q_pos[:, None, None] - key_pos).astype(F32)

    scores = []
    for i in range(n_sel):
        blk = sel[:, :, i]
        k_i = kb_h[kv_of_head[None, :], blk]
        key_pos = blk[:, :, None] * MOBA_BLOCK + offs
        s_i = _einsum32('thd,thjd->thj', q, k_i) * scale + alibi(key_pos)
        scores.append(jnp.where(keep[:, :, i, None], s_i, -jnp.inf))
    k_own = kb[own]
    v_own = vb[own]
    own_pos = own[:, None] * MOBA_BLOCK + offs
    s_own = _einsum32('tkgd,tjkd->tkgj', qg, k_own).reshape(tq, N_HEADS, MOBA_BLOCK) * scale
    s_own = s_own + alibi(own_pos[:, None, :])
    s_own = jnp.where((own_pos <= q_pos[:, None])[:, None, :], s_own, -jnp.inf)
    scores.append(s_own)
    p = jax.nn.softmax(jnp.concatenate(scores, axis=-1), axis=-1)
    p = p.reshape(tq, N_HEADS, n_sel + 1, MOBA_BLOCK).astype(v.dtype)
    p_own = p[:, :, n_sel].reshape(tq, N_KV_HEADS, KV_GROUP, MOBA_BLOCK)
    o = jnp.einsum('tkgj,tjkd->tkgd', p_own, v_own).reshape(tq, N_HEADS, HEAD_DIM)
    for i in range(n_sel):
        v_i = vb_h[kv_of_head[None, :], sel[:, :, i]]
        o = o + jnp.einsum('thj,thjd->thd', p[:, :, i], v_i)
    return o


def _sweep_prompt(core, q_tree, kv_tree):
    b, t = jax.tree_util.tree_leaves(q_tree)[0].shape[:2]
    nqb = t // Q_BLOCK
    to_blocks = lambda a: jnp.swapaxes(a.reshape(b, nqb, Q_BLOCK, *a.shape[2:]), 0, 1)
    q_blocks = jax.tree_util.tree_map(to_blocks, q_tree)
    q_pos = jnp.arange(t, dtype=jnp.int32).reshape(nqb, Q_BLOCK)
    k_pos = jnp.arange(t, dtype=jnp.int32)

    def body(args):
        qb, pb = args
        return jax.vmap(core, in_axes=(0, None, 0, None))(qb, pb, kv_tree, k_pos)

    out = jnp.swapaxes(lax.map(body, (q_blocks, q_pos)), 0, 1)
    return out.reshape(b, t, *out.shape[3:])


def _sweep_sample(core, q_tree, new_tree, cache_tree, layer_idx, page_table):
    s = jax.tree_util.tree_leaves(q_tree)[0].shape[1]
    past = page_table.shape[1] * PAGE_SIZE
    q_pos = past + jnp.arange(s, dtype=jnp.int32)
    k_pos = jnp.arange(past + s, dtype=jnp.int32)

    def body(args):
        qb, nb, pages = args

        def join(cache, new):
            rows = cache[layer_idx, pages]
            rows = rows.reshape(past, *cache.shape[3:]).astype(new.dtype)
            return jnp.concatenate([rows, new], axis=0)

        kv = jax.tree_util.tree_map(join, cache_tree, nb)
        return core(qb, q_pos, kv, k_pos)

    return lax.map(body, (q_tree, new_tree, page_table))


def _split_qkv(x, w):
    b, t = x.shape[:2]
    qkv = jnp.einsum('btd,de->bte', x, w)
    nq, nk = N_HEADS * HEAD_DIM, N_KV_HEADS * HEAD_DIM
    q = qkv[..., :nq].reshape(b, t, N_HEADS, HEAD_DIM)
    k = qkv[..., nq:nq + nk].reshape(b, t, N_KV_HEADS, HEAD_DIM)
    v = qkv[..., nq + nk:].reshape(b, t, N_KV_HEADS, HEAD_DIM)
    return q, k, v


def _out_proj(o, w):
    return jnp.einsum('bte,ed->btd', o.reshape(*o.shape[:2], -1), w)


def _stick_breaking_layer(xp, xs, w_qkv, w_o, cache_k, cache_v, j, page_table):
    qp, kp, vp = _split_qkv(xp, w_qkv)
    qs, ks, vs = _split_qkv(xs, w_qkv)
    op = _sweep_prompt(_sb_core, qp, (kp, vp))
    o_s = _sweep_sample(_sb_core, qs, (ks, vs), (cache_k, cache_v), j, page_table)
    return _out_proj(op, w_o), _out_proj(o_s, w_o), (kp, vp, ks, vs)


def _mla_project(x, pos, w_dq, q_norm, w_uq, w_dkv, kv_norm):
    b, t = x.shape[:2]
    cq = _rms_norm(jnp.einsum('btd,dr->btr', x, w_dq), q_norm)
    q = jnp.einsum('btr,re->bte', cq, w_uq).reshape(b, t, N_HEADS, MLA_NOPE + MLA_ROPE)
    cos, sin = _rope_angles(pos)
    q_nope = q[..., :MLA_NOPE]
    q_pe = _rope(q[..., MLA_NOPE:], cos[:, None, :], sin[:, None, :])
    dkv = jnp.einsum('btd,de->bte', x, w_dkv)
    c_kv = _rms_norm(dkv[..., :MLA_KV_RANK], kv_norm)
    k_pe = _rope(dkv[..., MLA_KV_RANK:], cos, sin)
    return q_nope, q_pe, c_kv, k_pe


def _mla_layer(xp, xs, pos_p, pos_s, w_dq, q_norm, w_uq, w_dkv, kv_norm, w_ukv, w_o,
               cache_ckv, cache_kpe, j, page_table):
    w_ukv = w_ukv.reshape(MLA_KV_RANK, N_HEADS, MLA_NOPE + MLA_V)
    w_uk, w_uv = w_ukv[..., :MLA_NOPE], w_ukv[..., MLA_NOPE:]
    qn_p, qr_p, c_p, kr_p = _mla_project(xp, pos_p, w_dq, q_norm, w_uq, w_dkv, kv_norm)
    k_nope = jnp.einsum('btc,chd->bthd', c_p, w_uk)
    v_p = jnp.einsum('btc,chd->bthd', c_p, w_uv)
    op = _sweep_prompt(_mla_core_expanded, (qn_p, qr_p), (k_nope, kr_p, v_p))
    qn_s, qr_s, c_s, kr_s = _mla_project(xs, pos_s, w_dq, q_norm, w_uq, w_dkv, kv_norm)
    q_lat = jnp.einsum('bthd,chd->bthc', qn_s, w_uk)
    o_lat = _sweep_sample(_mla_core_latent, (q_lat, qr_s), (c_s, kr_s), (cache_ckv, cache_kpe), j, page_table)
    o_s = jnp.einsum('bthc,chd->bthd', o_lat, w_uv)
    return _out_proj(op, w_o), _out_proj(o_s, w_o), (c_p, kr_p, c_s, kr_s)


def _fox_layer(xp, xs, w_qkv, w_f, b_f, w_o, cache_k, cache_v, cache_logf, j, page_table):
    def project(x):
        q, k, v = _split_qkv(x, w_qkv)
        logf = jax.nn.log_sigmoid(_einsum32('btd,dh->bth', x, w_f) + b_f.astype(F32))
        return q, k, v, logf

    qp, kp, vp, fp = project(xp)
    qs, ks, vs, fs = project(xs)
    op = _sweep_prompt(_fox_core, qp, (kp, vp, fp))
    o_s = _sweep_sample(_fox_core, qs, (ks, vs, fs), (cache_k, cache_v, cache_logf), j, page_table)
    return _out_proj(op, w_o), _out_proj(o_s, w_o), (kp, vp, fp, ks, vs, fs)


def _moba_layer(xp, xs, w_qkv, w_o, cache_k, cache_v, j, page_table):
    qp, kp, vp = _split_qkv(xp, w_qkv)
    qs, ks, vs = _split_qkv(xs, w_qkv)
    op = _sweep_prompt(_moba_core, qp, (kp, vp))
    o_s = _sweep_sample(_moba_core, qs, (ks, vs), (cache_k, cache_v), j, page_table)
    return _out_proj(op, w_o), _out_proj(o_s, w_o), (kp, vp, ks, vs)


def _moe(x, router_w, router_b, w_gate, w_up, w_down):
    shape = x.shape
    xt = x.reshape(-1, shape[-1])
    n = xt.shape[0]
    aff = jax.nn.sigmoid(_einsum32('nd,de->ne', xt, router_w))
    choice = aff + router_b.astype(F32)
    grp_score = jnp.sum(lax.top_k(choice.reshape(n, N_GROUPS, EXPERTS_PER_GROUP), GROUP_SCORE_K)[0], -1)
    grp = jnp.argmax(grp_score, -1)
    in_grp = (jnp.arange(N_EXPERTS) // EXPERTS_PER_GROUP)[None, :] == grp[:, None]
    _, idx = lax.top_k(jnp.where(in_grp, choice, -jnp.inf), TOP_K)
    w = jnp.take_along_axis(aff, idx, -1)
    w = w / jnp.sum(w, -1, keepdims=True)
    gates = jnp.sum(jax.nn.one_hot(idx, N_EXPERTS, dtype=F32) * w[..., None], 1)
    h = jax.nn.silu(jnp.einsum('nd,edf->nef', xt, w_gate)) * jnp.einsum('nd,edf->nef', xt, w_up)
    y = jnp.einsum('nef,efd->nd', h * gates[:, :, None].astype(h.dtype), w_down)
    return y.reshape(shape)


def setup_inputs(seed: int = 0) -> dict:
    key = jax.random.key(seed)
    ks = iter(jax.random.split(key, 64))

    def nrm(shape, scale=1.0):
        return jax.random.normal(next(ks), shape, F32) * scale

    n_pages = PAST_LEN // PAGE_SIZE
    n_used = DEC_BATCH * n_pages
    n_pool = n_used + max(1, n_used // 4)
    page_table = jax.random.permutation(next(ks), n_pool)[:n_used].reshape(DEC_BATCH, n_pages).astype(jnp.int32)
    pool = (n_pool, PAGE_SIZE)
    hd = N_HEADS * HEAD_DIM
    return {
        'x_prompt': nrm((BATCH, SEQ, D_MODEL)),
        'x_sample': nrm((DEC_BATCH, DEC_SEQ, D_MODEL)),
        'cache_sb_k': nrm((N_SB_LAYERS, *pool, N_KV_HEADS, HEAD_DIM)),
        'cache_sb_v': nrm((N_SB_LAYERS, *pool, N_KV_HEADS, HEAD_DIM)),
        'cache_mla_ckv': nrm((N_MLA_LAYERS, *pool, MLA_KV_RANK)),
        'cache_mla_kpe': nrm((N_MLA_LAYERS, *pool, MLA_ROPE)),
        'cache_fox_k': nrm((N_FOX_LAYERS, *pool, N_KV_HEADS, HEAD_DIM)),
        'cache_fox_v': nrm((N_FOX_LAYERS, *pool, N_KV_HEADS, HEAD_DIM)),
        'cache_fox_logf': jax.nn.log_sigmoid(FORGET_BIAS + nrm((N_FOX_LAYERS, *pool, N_HEADS))),
        'cache_moba_k': nrm((N_MOBA_LAYERS, *pool, N_KV_HEADS, HEAD_DIM)),
        'cache_moba_v': nrm((N_MOBA_LAYERS, *pool, N_KV_HEADS, HEAD_DIM)),
        'page_table': page_table,
        'ln_g': 1.0 + nrm((DEPTH, 2, D_MODEL), 0.02),
        'ln_b': nrm((DEPTH, 2, D_MODEL), 0.02),
        'sb_w_qkv': nrm((N_SB_LAYERS, D_MODEL, QKV_WIDTH), D_MODEL ** -0.5),
        'sb_w_o': nrm((N_SB_LAYERS, hd, D_MODEL), hd ** -0.5 * DN_BETA),
        'mla_w_dq': nrm((N_MLA_LAYERS, D_MODEL, MLA_Q_RANK), D_MODEL ** -0.5),
        'mla_q_norm': 1.0 + nrm((N_MLA_LAYERS, MLA_Q_RANK), 0.02),
        'mla_w_uq': nrm((N_MLA_LAYERS, MLA_Q_RANK, N_HEADS * (MLA_NOPE + MLA_ROPE)), MLA_Q_RANK ** -0.5),
        'mla_w_dkv': nrm((N_MLA_LAYERS, D_MODEL, MLA_KV_RANK + MLA_ROPE), D_MODEL ** -0.5),
        'mla_kv_norm': 1.0 + nrm((N_MLA_LAYERS, MLA_KV_RANK), 0.02),
        'mla_w_ukv': nrm((N_MLA_LAYERS, MLA_KV_RANK, N_HEADS * (MLA_NOPE + MLA_V)), MLA_KV_RANK ** -0.5),
        'mla_w_o': nrm((N_MLA_LAYERS, N_HEADS * MLA_V, D_MODEL), (N_HEADS * MLA_V) ** -0.5 * DN_BETA),
        'fox_w_qkv': nrm((N_FOX_LAYERS, D_MODEL, QKV_WIDTH), D_MODEL ** -0.5),
        'fox_w_f': nrm((N_FOX_LAYERS, D_MODEL, N_HEADS), D_MODEL ** -0.5),
        'fox_b_f': FORGET_BIAS + nrm((N_FOX_LAYERS, N_HEADS), 0.5),
        'fox_w_o': nrm((N_FOX_LAYERS, hd, D_MODEL), hd ** -0.5 * DN_BETA),
        'moba_w_qkv': nrm((N_MOBA_LAYERS, D_MODEL, QKV_WIDTH), D_MODEL ** -0.5),
        'moba_w_o': nrm((N_MOBA_LAYERS, hd, D_MODEL), hd ** -0.5 * DN_BETA),
        'router_w': nrm((D_MODEL, N_EXPERTS), D_MODEL ** -0.5),
        'router_b': nrm((N_EXPERTS,), 0.01),
        'moe_w_gate': nrm((DEPTH, N_EXPERTS, D_MODEL, D_EXPERT), D_MODEL ** -0.5),
        'moe_w_up': nrm((DEPTH, N_EXPERTS, D_MODEL, D_EXPERT), D_MODEL ** -0.5),
        'moe_w_down': nrm((DEPTH, N_EXPERTS, D_EXPERT, D_MODEL), D_EXPERT ** -0.5 * DN_BETA),
    }


def reference(x_prompt, x_sample, cache_sb_k, cache_sb_v, cache_mla_ckv, cache_mla_kpe,
              cache_fox_k, cache_fox_v, cache_fox_logf, cache_moba_k, cache_moba_v, page_table,
              ln_g, ln_b, sb_w_qkv, sb_w_o, mla_w_dq, mla_q_norm, mla_w_uq, mla_w_dkv, mla_kv_norm,
              mla_w_ukv, mla_w_o, fox_w_qkv, fox_w_f, fox_b_f, fox_w_o, moba_w_qkv, moba_w_o,
              router_w, router_b, moe_w_gate, moe_w_up, moe_w_down):
    past = page_table.shape[1] * PAGE_SIZE
    pos_p = jnp.arange(x_prompt.shape[1], dtype=jnp.int32)
    pos_s = past + jnp.arange(x_sample.shape[1], dtype=jnp.int32)
    xp, xs = x_prompt, x_sample
    sb_st, mla_st, fox_st, moba_st = [], [], [], []
    for layer in range(DEPTH):
        kind, j = layer % N_MIXERS, layer // N_MIXERS
        if kind == 0:
            hp, hs, st = _stick_breaking_layer(xp, xs, sb_w_qkv[j], sb_w_o[j], cache_sb_k, cache_sb_v, j, page_table)
            sb_st.append(st)
        elif kind == 1:
            hp, hs, st = _mla_layer(xp, xs, pos_p, pos_s, mla_w_dq[j], mla_q_norm[j], mla_w_uq[j], mla_w_dkv[j],
                                    mla_kv_norm[j], mla_w_ukv[j], mla_w_o[j], cache_mla_ckv, cache_mla_kpe, j, page_table)
            mla_st.append(st)
        elif kind == 2:
            hp, hs, st = _fox_layer(xp, xs, fox_w_qkv[j], fox_w_f[j], fox_b_f[j], fox_w_o[j],
                                    cache_fox_k, cache_fox_v, cache_fox_logf, j, page_table)
            fox_st.append(st)
        else:
            hp, hs, st = _moba_layer(xp, xs, moba_w_qkv[j], moba_w_o[j], cache_moba_k, cache_moba_v, j, page_table)
            moba_st.append(st)
        xp = _post_norm(xp, hp, ln_g[layer, 0], ln_b[layer, 0])
        xs = _post_norm(xs, hs, ln_g[layer, 0], ln_b[layer, 0])
        mp = _moe(xp, router_w, router_b, moe_w_gate[layer], moe_w_up[layer], moe_w_down[layer])
        ms = _moe(xs, router_w, router_b, moe_w_gate[layer], moe_w_up[layer], moe_w_down[layer])
        xp = _post_norm(xp, mp, ln_g[layer, 1], ln_b[layer, 1])
        xs = _post_norm(xs, ms, ln_g[layer, 1], ln_b[layer, 1])

    def stack(states, i):
        return jnp.stack([s[i] for s in states])

    return (xp, xs,
            stack(sb_st, 0), stack(sb_st, 1), stack(sb_st, 2), stack(sb_st, 3),
            stack(mla_st, 0), stack(mla_st, 1), stack(mla_st, 2), stack(mla_st, 3),
            stack(fox_st, 0), stack(fox_st, 1), stack(fox_st, 2), stack(fox_st, 3), stack(fox_st, 4), stack(fox_st, 5),
            stack(moba_st, 0), stack(moba_st, 1), stack(moba_st, 2), stack(moba_st, 3))
```

```python
import functools

import jax
import jax.numpy as jnp
from jax import lax
from jax.experimental import pallas as pl
from jax.experimental.pallas import tpu as pltpu

F32 = jnp.float32
BF16 = jnp.bfloat16

HEAD_DIM = 128
N_HEADS = 16
N_KV_HEADS = 2
KV_GROUP = N_HEADS // N_KV_HEADS
PAGE_SIZE = 128
MLA_NOPE = 128
MLA_ROPE = 64
MLA_V = 128
MLA_SCALE = (MLA_NOPE + MLA_ROPE) ** -0.5
ROPE_THETA = 10000.0
MOBA_BLOCK = 256
MOBA_TOPK = 3
N_EXPERTS = 16
EXPERTS_PER_GROUP = 4
N_GROUPS = N_EXPERTS // EXPERTS_PER_GROUP
DEPTH = 4
DN_ALPHA = (2 * DEPTH) ** 0.25
LN_EPS = 1e-5
RMS_EPS = 1e-6
ATT_SCALE = HEAD_DIM ** -0.5

LANES = 128
MASKED = -1e30
VMEM_LIMIT_BYTES = 56 * 1024 * 1024

NT_DIMS = (((1,), (1,)), ((), ()))


def _cparams(semantics):
    return pltpu.CompilerParams(dimension_semantics=semantics, vmem_limit_bytes=VMEM_LIMIT_BYTES)


def _dot(a, b):
    return jnp.dot(a, b, preferred_element_type=F32)


def _dot_nt(a, b):
    return lax.dot_general(a, b, NT_DIMS, preferred_element_type=F32)


def _split_bf16(x):
    hi = x.astype(BF16)
    lo = (x - hi.astype(F32)).astype(BF16)
    return hi, lo


def _dot_nt_precise(a, b):
    a_hi, a_lo = _split_bf16(a)
    b_hi, b_lo = _split_bf16(b)
    return _dot_nt(a_hi, b_hi) + _dot_nt(a_lo, b_hi) + _dot_nt(a_hi, b_lo)


def _log_sigmoid_pair(z):
    sp = jnp.log1p(jnp.exp(-jnp.abs(z)))
    return jnp.minimum(z, 0.0) - sp, -jnp.maximum(z, 0.0) - sp


def _suffix_exclusive(x, u):
    hi, lo = _split_bf16(x)
    return _dot(hi, u) + _dot(lo, u)


def _layer_norm(y, g, b):
    mean = jnp.mean(y, axis=-1, keepdims=True)
    yc = y - mean
    var = jnp.mean(yc * yc, axis=-1, keepdims=True)
    return yc * lax.rsqrt(var + LN_EPS) * g + b


def _rms_norm(y, g):
    return y * lax.rsqrt(jnp.mean(y * y, axis=-1, keepdims=True) + RMS_EPS) * g


def _stack_heads(ref, n):
    return jnp.concatenate([ref[:, h * HEAD_DIM:(h + 1) * HEAD_DIM] for h in range(n)], axis=0)


def _softmax_update(s, valid, v, m_ref, l_ref, acc_ref):
    m_old = m_ref[...]
    m_new = jnp.maximum(m_old, jnp.max(s, axis=1, keepdims=True))
    alpha = jnp.exp(m_old - m_new)
    p = jnp.exp(s - m_new)
    if valid is not None:
        p = jnp.where(valid, p, 0.0)
    l_ref[...] = alpha * l_ref[...] + jnp.sum(p, axis=1, keepdims=True)
    acc_ref[...] = alpha * acc_ref[...] + _dot(p.astype(BF16), v)
    m_ref[...] = m_new


def _fused_matmul_kernel(*refs, n_row, n_const, epilogue):
    x_ref, w_ref = refs[0], refs[1]
    row_refs = refs[2:2 + n_row]
    const_refs = refs[2 + n_row:2 + n_row + n_const]
    out_refs = refs[2 + n_row + n_const:]
    acc = _dot(x_ref[...].astype(BF16), w_ref[...])
    outs = epilogue(acc, [r[...] for r in row_refs], [c[...] for c in const_refs])
    for o_ref, val in zip(out_refs, outs):
        o_ref[...] = val.astype(o_ref.dtype)


def _fused_matmul(x, w, epilogue, outs, row_ins=(), const_ins=(), tm=256, name="fused_matmul"):
    m, k = x.shape
    n = w.shape[1]
    assert m % tm == 0 and w.shape[0] == k
    in_specs = [pl.BlockSpec((tm, k), lambda i: (i, 0)),
                pl.BlockSpec((k, n), lambda i: (0, 0), pipeline_mode=pl.Buffered(1))]
    in_specs += [pl.BlockSpec((tm, a.shape[1]), lambda i: (i, 0)) for a in row_ins]
    in_specs += [pl.BlockSpec(c.shape, lambda i: (0, 0)) for c in const_ins]
    out_shape = [jax.ShapeDtypeStruct((m, width), dt) for width, dt in outs]
    out_specs = [pl.BlockSpec((tm, width), lambda i: (i, 0)) for width, _ in outs]
    return pl.pallas_call(
        functools.partial(_fused_matmul_kernel, n_row=len(row_ins), n_const=len(const_ins),
                          epilogue=epilogue),
        grid=(m // tm,), in_specs=in_specs, out_specs=out_specs, out_shape=out_shape,
        compiler_params=_cparams(("arbitrary",)), name=name,
    )(x, w, *row_ins, *const_ins)


def _bmm_kernel(x_ref, w_ref, o_ref):
    o_ref[...] = _dot(x_ref[...].astype(BF16), w_ref[...].astype(BF16)).astype(o_ref.dtype)


def _batched_matmul(x, w, out_dtype, name):
    h, m, k = x.shape
    n = w.shape[2]
    return pl.pallas_call(
        _bmm_kernel, grid=(h,),
        in_specs=[pl.BlockSpec((None, m, k), lambda i: (i, 0, 0)),
                  pl.BlockSpec((None, k, n), lambda i: (i, 0, 0))],
        out_specs=pl.BlockSpec((None, m, n), lambda i: (i, 0, 0)),
        out_shape=jax.ShapeDtypeStruct((h, m, n), out_dtype),
        compiler_params=_cparams(("arbitrary",)), name=name,
    )(x, w)


def _qkv_epilogue(acc, rows, consts):
    nq = N_HEADS * HEAD_DIM
    nk = N_KV_HEADS * HEAD_DIM
    return acc[:, :nq], acc[:, nq:nq + nk], acc[:, nq + nk:nq + 2 * nk]


def _qkv_fox_epilogue(acc, rows, consts):
    nq = N_HEADS * HEAD_DIM
    nk = N_KV_HEADS * HEAD_DIM
    (b_f,) = consts
    logf, _ = _log_sigmoid_pair(acc[:, nq + 2 * nk:] + b_f)
    return acc[:, :nq], acc[:, nq:nq + nk], acc[:, nq + nk:nq + 2 * nk], logf


def _mla_down_epilogue(acc, rows, consts):
    cos, sin = rows
    q_norm, kv_norm = consts
    r = q_norm.shape[1]
    c = kv_norm.shape[1]
    cq = _rms_norm(acc[:, :r], q_norm)
    ckv = _rms_norm(acc[:, r:r + c], kv_norm)
    kpe = acc[:, r + c:r + c + LANES] * cos + acc[:, r + c + LANES:] * sin
    return cq, ckv, kpe


def _mla_q_epilogue(acc, rows, consts):
    cos, sin = rows
    n = N_HEADS * HEAD_DIM
    cos_t = jnp.concatenate([cos] * N_HEADS, axis=1)
    sin_t = jnp.concatenate([sin] * N_HEADS, axis=1)
    return acc[:, :n], acc[:, n:2 * n] * cos_t + acc[:, 2 * n:] * sin_t


def _plain_epilogue(acc, rows, consts):
    return (acc,)


def _route(x1, rw_hi, rw_lo, rb):
    x_hi, x_lo = _split_bf16(x1)
    logits = _dot_nt(rw_hi, x_hi) + _dot_nt(rw_hi, x_lo) + _dot_nt(rw_lo, x_hi)
    aff = 1.0 / (1.0 + jnp.exp(-logits))
    choice = aff + rb
    c = [choice[e:e + 1, :] for e in range(N_EXPERTS)]
    a = [aff[e:e + 1, :] for e in range(N_EXPERTS)]
    scores = []
    for g in range(N_GROUPS):
        cg = c[g * EXPERTS_PER_GROUP:(g + 1) * EXPERTS_PER_GROUP]
        best = None
        for i in range(EXPERTS_PER_GROUP):
            for j in range(i + 1, EXPERTS_PER_GROUP):
                pair = cg[i] + cg[j]
                best = pair if best is None else jnp.maximum(best, pair)
        scores.append(best)
    grp = jnp.zeros_like(scores[0], dtype=jnp.int32)
    best = scores[0]
    for g in range(1, N_GROUPS):
        better = scores[g] > best
        grp = jnp.where(better, g, grp)
        best = jnp.where(better, scores[g], best)
    w = []
    for e in range(N_EXPERTS):
        g = e // EXPERTS_PER_GROUP
        rank = jnp.zeros_like(grp)
        for e2 in range(g * EXPERTS_PER_GROUP, (g + 1) * EXPERTS_PER_GROUP):
            if e2 == e:
                continue
            beats = (c[e2] >= c[e]) if e2 < e else (c[e2] > c[e])
            rank = rank + beats.astype(jnp.int32)
        selected = (grp == g) & (rank < 2)
        w.append(jnp.where(selected, a[e], 0.0))
    denom = w[0]
    for e in range(1, N_EXPERTS):
        denom = denom + w[e]
    return jnp.concatenate([we / denom for we in w], axis=0)


def _oproj_norm_route_kernel(o_ref, w_ref, x_ref, g_ref, b_ref, rwh_ref, rwl_ref, rb_ref,
                             x1_ref, gates_ref):
    h = _dot(o_ref[...].astype(BF16), w_ref[...])
    x1 = _layer_norm(DN_ALPHA * x_ref[...] + h, g_ref[...], b_ref[...])
    x1_ref[...] = x1
    gates_ref[...] = _route(x1, rwh_ref[...], rwl_ref[...], rb_ref[...])


def _oproj_norm_route(o, w_o, x, ln_g, ln_b, rw_hi, rw_lo, rb, tm=256):
    m, d = x.shape
    k = o.shape[1]
    const = lambda shape: pl.BlockSpec(shape, lambda i: (0, 0))
    return pl.pallas_call(
        _oproj_norm_route_kernel, grid=(m // tm,),
        in_specs=[pl.BlockSpec((tm, k), lambda i: (i, 0)),
                  pl.BlockSpec((k, d), lambda i: (0, 0), pipeline_mode=pl.Buffered(1)),
                  pl.BlockSpec((tm, d), lambda i: (i, 0)),
                  const((1, d)), const((1, d)), const(rw_hi.shape), const(rw_lo.shape), const(rb.shape)],
        out_specs=[pl.BlockSpec((tm, d), lambda i: (i, 0)),
                   pl.BlockSpec((N_EXPERTS, tm), lambda i: (0, i))],
        out_shape=[jax.ShapeDtypeStruct((m, d), F32), jax.ShapeDtypeStruct((N_EXPERTS, m), F32)],
        compiler_params=_cparams(("arbitrary",)), name="oproj_norm_route",
    )(o, w_o, x, ln_g, ln_b, rw_hi, rw_lo, rb)


def _moe_kernel(x_ref, gates_ref, wg_ref, wu_ref, wd_ref, g_ref, b_ref, o_ref, xb_ref, acc_ref):
    e = pl.program_id(1)

    @pl.when(e == 0)
    def _():
        xb_ref[...] = x_ref[...].astype(BF16)
        acc_ref[...] = jnp.zeros_like(acc_ref)

    xb = xb_ref[...]
    gates = gates_ref[...]
    lane = lax.broadcasted_iota(jnp.int32, gates.shape, 1)
    gate = jnp.sum(jnp.where(lane == e, gates, 0.0), axis=1, keepdims=True)
    hg = _dot(xb, wg_ref[...])
    hu = _dot(xb, wu_ref[...])
    h = hg / (1.0 + jnp.exp(-hg)) * hu * gate
    acc_ref[...] += _dot(h.astype(BF16), wd_ref[...])

    @pl.when(e == N_EXPERTS - 1)
    def _():
        o_ref[...] = _layer_norm(DN_ALPHA * x_ref[...] + acc_ref[...], g_ref[...], b_ref[...])


def _moe_norm(x1, gates, w_gate, w_up, w_down, ln_g, ln_b, tm):
    m, d = x1.shape
    f = w_gate.shape[2]
    return pl.pallas_call(
        _moe_kernel, grid=(m // tm, N_EXPERTS),
        in_specs=[pl.BlockSpec((tm, d), lambda i, e: (i, 0)),
                  pl.BlockSpec((tm, N_EXPERTS), lambda i, e: (i, 0)),
                  pl.BlockSpec((None, d, f), lambda i, e: (e, 0, 0)),
                  pl.BlockSpec((None, d, f), lambda i, e: (e, 0, 0)),
                  pl.BlockSpec((None, f, d), lambda i, e: (e, 0, 0)),
                  pl.BlockSpec((1, d), lambda i, e: (0, 0)),
                  pl.BlockSpec((1, d), lambda i, e: (0, 0))],
        out_specs=pl.BlockSpec((tm, d), lambda i, e: (i, 0)),
        out_shape=jax.ShapeDtypeStruct((m, d), F32),
        scratch_shapes=[pltpu.VMEM((tm, d), BF16), pltpu.VMEM((tm, d), F32)],
        compiler_params=_cparams(("arbitrary", "arbitrary")), name="moe_norm",
    )(x1, gates, w_gate, w_up, w_down, ln_g, ln_b)


PROMPT_TQ = 128
PROMPT_TK = 256


def _row_positions(qi, tq):
    r = lax.broadcasted_iota(jnp.int32, (KV_GROUP * tq, 1), 0)
    return qi * tq + (r % tq)


def _load_kv(k_ref, v_ref, j, tk):
    start = pl.multiple_of(j * tk, tk)
    return k_ref[pl.ds(start, tk), :].astype(BF16), v_ref[pl.ds(start, tk), :].astype(BF16)


def _store_heads(o_ref, o, tq):
    for g in range(KV_GROUP):
        o_ref[:, g * HEAD_DIM:(g + 1) * HEAD_DIM] = o[g * tq:(g + 1) * tq].astype(o_ref.dtype)


def _sb_prompt_kernel(q_ref, k_ref, v_ref, u_ref, o_ref, run_ref, acc_ref, *, tq, tk):
    qi = pl.program_id(2)
    q = _stack_heads(q_ref, KV_GROUP).astype(BF16)
    tpos = _row_positions(qi, tq)
    run_ref[...] = jnp.zeros_like(run_ref)
    acc_ref[...] = jnp.zeros_like(acc_ref)
    n_chunks = (qi * tq + tq - 1) // tk + 1

    def body(jj, carry):
        j = n_chunks - 1 - jj
        k, v = _load_kv(k_ref, v_ref, j, tk)
        z = _dot_nt(q, k) * ATT_SCALE
        kpos = j * tk + lax.broadcasted_iota(jnp.int32, (1, tk), 1)
        past = kpos < tpos
        ls, lnk = _log_sigmoid_pair(z)
        lk = jnp.where(past, lnk, 0.0)
        between = _suffix_exclusive(lk, u_ref[...]) + run_ref[...]
        w = jnp.where(past, jnp.exp(ls + between), 0.0)
        acc_ref[...] += _dot(w.astype(BF16), v)
        run_ref[...] += jnp.sum(lk, axis=1, keepdims=True)
        return carry

    lax.fori_loop(0, n_chunks, body, 0)
    _store_heads(o_ref, acc_ref[...], tq)


def _fox_prompt_kernel(q_ref, k_ref, v_ref, c_ref, ct_ref, o_ref, m_ref, l_ref, acc_ref, *, tq, tk):
    qi = pl.program_id(2)
    q = _stack_heads(q_ref, KV_GROUP).astype(BF16)
    tpos = _row_positions(qi, tq)
    c_blk = c_ref[...]
    cq = jnp.concatenate([c_blk[:, g:g + 1] for g in range(KV_GROUP)], axis=0)
    m_ref[...] = jnp.full_like(m_ref, MASKED)
    l_ref[...] = jnp.zeros_like(l_ref)
    acc_ref[...] = jnp.zeros_like(acc_ref)
    n_chunks = (qi * tq + tq - 1) // tk + 1

    def body(j, carry):
        k, v = _load_kv(k_ref, v_ref, j, tk)
        start = pl.multiple_of(j * tk, tk)
        ck = ct_ref[:, pl.ds(start, tk)]
        ck = jnp.concatenate([jnp.broadcast_to(ck[g:g + 1, :], (tq, tk)) for g in range(KV_GROUP)], axis=0)
        s = _dot_nt(q, k) * ATT_SCALE + cq - ck
        kpos = j * tk + lax.broadcasted_iota(jnp.int32, (1, tk), 1)
        valid = kpos <= tpos
        s = jnp.where(valid, s, MASKED)
        _softmax_update(s, valid, v, m_ref, l_ref, acc_ref)
        return carry

    lax.fori_loop(0, n_chunks, body, 0)
    _store_heads(o_ref, acc_ref[...] / l_ref[...], tq)


def _moba_select(gate, n_blocks_past, n_cols):
    lane = lax.broadcasted_iota(jnp.int32, gate.shape, 1)
    is_past = lane < n_blocks_past
    gate = jnp.where(is_past, gate, -jnp.inf)
    rank = jnp.zeros(gate.shape, jnp.int32)
    for b2 in range(n_cols):
        g2 = gate[:, b2:b2 + 1]
        beats = (g2 > gate) | ((g2 == gate) & (lane > b2))
        rank = rank + beats.astype(jnp.int32)
    return (is_past & (rank < MOBA_TOPK) & (gate > -jnp.inf)).astype(F32)


def _alibi_slopes_rows(kvh, rows_per_head):
    r = lax.broadcasted_iota(jnp.int32, (KV_GROUP * rows_per_head, 1), 0)
    h = kvh * KV_GROUP + r // rows_per_head
    return jnp.exp2(-8.0 * (h + 1).astype(F32) / N_HEADS)


def _moba_prompt_kernel(q_ref, k_ref, v_ref, o_ref, kmean_ref, m_ref, l_ref, acc_ref, *, tq, tk, n_blocks):
    kvh = pl.program_id(1)
    qi = pl.program_id(2)

    @pl.when(qi == 0)
    def _():
        kmean_ref[...] = jnp.zeros_like(kmean_ref)
        for b in range(n_blocks):
            blk = k_ref[b * MOBA_BLOCK:(b + 1) * MOBA_BLOCK, :]
            kmean_ref[b:b + 1, :] = jnp.sum(blk, axis=0, keepdims=True) * (1.0 / MOBA_BLOCK)

    qf = _stack_heads(q_ref, KV_GROUP)
    q = qf.astype(BF16)
    tpos = _row_positions(qi, tq)
    own = (qi * tq) // MOBA_BLOCK
    sel = _moba_select(_dot_nt_precise(qf, kmean_ref[...]), own, n_blocks)
    sel_lane = lax.broadcasted_iota(jnp.int32, sel.shape, 1)
    slope = _alibi_slopes_rows(kvh, tq)
    m_ref[...] = jnp.full_like(m_ref, MASKED)
    l_ref[...] = jnp.zeros_like(l_ref)
    acc_ref[...] = jnp.zeros_like(acc_ref)

    def body(j, carry):
        k, v = _load_kv(k_ref, v_ref, j, tk)
        kpos = j * tk + lax.broadcasted_iota(jnp.int32, (1, tk), 1)
        s = _dot_nt(q, k) * ATT_SCALE - slope * (tpos - kpos).astype(F32)
        picked = jnp.sum(jnp.where(sel_lane == j, sel, 0.0), axis=1, keepdims=True) > 0.0
        valid = jnp.logical_or(jnp.logical_and(picked, j < own), jnp.logical_and(kpos <= tpos, j == own))
        s = jnp.where(valid, s, MASKED)
        _softmax_update(s, valid, v, m_ref, l_ref, acc_ref)
        return carry

    lax.fori_loop(0, own + 1, body, 0)
    _store_heads(o_ref, acc_ref[...] / l_ref[...], tq)


def _gqa_prompt_attention(kind, q, k, v, batch, seq, extra=()):
    tq, tk = PROMPT_TQ, PROMPT_TK
    nq = seq // tq
    rows = KV_GROUP * tq
    gw = KV_GROUP * HEAD_DIM
    q_spec = pl.BlockSpec((tq, gw), lambda b, h, i: (b * nq + i, h))
    kv_spec = pl.BlockSpec((seq, HEAD_DIM), lambda b, h, i: (b, h))
    o_spec = pl.BlockSpec((tq, gw), lambda b, h, i: (b * nq + i, h))
    col = pltpu.VMEM((rows, 1), F32)
    acc = pltpu.VMEM((rows, HEAD_DIM), F32)
    if kind == "sb":
        (u,) = extra
        kern = functools.partial(_sb_prompt_kernel, tq=tq, tk=tk)
        in_specs = [q_spec, kv_spec, kv_spec, pl.BlockSpec(u.shape, lambda b, h, i: (0, 0))]
        scratch = [col, acc]
    elif kind == "fox":
        c, ct = extra
        kern = functools.partial(_fox_prompt_kernel, tq=tq, tk=tk)
        in_specs = [q_spec, kv_spec, kv_spec,
                    pl.BlockSpec((None, None, tq, KV_GROUP), lambda b, h, i: (b, h, i, 0)),
                    pl.BlockSpec((None, None, KV_GROUP, seq), lambda b, h, i: (b, h, 0, 0))]
        scratch = [col, col, acc]
    else:
        kern = functools.partial(_moba_prompt_kernel, tq=tq, tk=tk, n_blocks=seq // MOBA_BLOCK)
        in_specs = [q_spec, kv_spec, kv_spec]
        scratch = [pltpu.VMEM((LANES, HEAD_DIM), F32), col, col, acc]
    return pl.pallas_call(
        kern, grid=(batch, N_KV_HEADS, nq), in_specs=in_specs, out_specs=o_spec,
        out_shape=jax.ShapeDtypeStruct((batch * seq, N_HEADS * HEAD_DIM), BF16),
        scratch_shapes=scratch,
        compiler_params=_cparams(("arbitrary", "arbitrary", "arbitrary")), name=kind + "_prompt",
    )(q, k, v, *extra)


MLA_TQ = 256
MLA_TK = 256


def _mla_prompt_kernel(qn_ref, qr_ref, kv_k_ref, kv_v_ref, kr_ref, o_ref, m_ref, l_ref, acc_ref, *, tq, tk):
    qi = pl.program_id(2)
    qn = qn_ref[...]
    qr = qr_ref[...]
    tpos = qi * tq + lax.broadcasted_iota(jnp.int32, (tq, 1), 0)
    m_ref[...] = jnp.full_like(m_ref, MASKED)
    l_ref[...] = jnp.zeros_like(l_ref)
    acc_ref[...] = jnp.zeros_like(acc_ref)
    n_chunks = (qi * tq + tq - 1) // tk + 1

    def body(j, carry):
        start = pl.multiple_of(j * tk, tk)
        kn = kv_k_ref[pl.ds(start, tk), :]
        v = kv_v_ref[pl.ds(start, tk), :]
        kr = kr_ref[pl.ds(start, tk), :].astype(BF16)
        s = (_dot_nt(qn, kn) + _dot_nt(qr, kr)) * MLA_SCALE
        kpos = j * tk + lax.broadcasted_iota(jnp.int32, (1, tk), 1)
        valid = kpos <= tpos
        s = jnp.where(valid, s, MASKED)
        _softmax_update(s, valid, v, m_ref, l_ref, acc_ref)
        return carry

    lax.fori_loop(0, n_chunks, body, 0)
    o_ref[...] = (acc_ref[...] / l_ref[...]).astype(o_ref.dtype)


def _mla_prompt_attention(q_nope, q_pe, kv_up, k_pe, batch, seq):
    tq, tk = MLA_TQ, MLA_TK
    nq = seq // tq
    q_spec = pl.BlockSpec((tq, HEAD_DIM), lambda b, h, i: (b * nq + i, h))
    col = pltpu.VMEM((tq, 1), F32)
    return pl.pallas_call(
        functools.partial(_mla_prompt_kernel, tq=tq, tk=tk), grid=(batch, N_HEADS, nq),
        in_specs=[q_spec, q_spec,
                  pl.BlockSpec((seq, HEAD_DIM), lambda b, h, i: (b, 2 * h)),
                  pl.BlockSpec((seq, HEAD_DIM), lambda b, h, i: (b, 2 * h + 1)),
                  pl.BlockSpec((seq, LANES), lambda b, h, i: (b, 0))],
        out_specs=q_spec,
        out_shape=jax.ShapeDtypeStruct((batch * seq, N_HEADS * MLA_V), BF16),
        scratch_shapes=[col, col, pltpu.VMEM((tq, MLA_V), F32)],
        compiler_params=_cparams(("arbitrary", "arbitrary", "arbitrary")), name="mla_prompt",
    )(q_nope, q_pe, kv_up, kv_up, k_pe)


PAGES_PER_STEP = 16


def _page_index_map(layer, n_pages, n_groups, p, trailing):
    def index_map(s, g, pt):
        slot = (n_groups - 1 - g) * PAGES_PER_STEP + p
        return (layer, pt[s * n_pages + slot]) + (0,) * trailing
    return index_map


def _page_specs(block, layer, n_pages, n_groups):
    return [pl.BlockSpec((None, None) + block, _page_index_map(layer, n_pages, n_groups, p, len(block)))
            for p in range(PAGES_PER_STEP)]


def _kv_page_head(ref, kvh):
    return ref[pl.ds(kvh, PAGE_SIZE, stride=N_KV_HEADS), :].astype(BF16)


def _gather_kv(page_refs, kvh):
    return jnp.concatenate([_kv_page_head(r, kvh) for r in page_refs], axis=0)


def _sample_row_query(n_q):
    r = lax.broadcasted_iota(jnp.int32, (n_q * KV_GROUP, 1), 0)
    return r // KV_GROUP


def _sb_sample_kernel(pt_ref, q_ref, kn_ref, vn_ref, u_ref, *rest, n_q):
    k_pages = rest[:PAGES_PER_STEP]
    v_pages = rest[PAGES_PER_STEP:2 * PAGES_PER_STEP]
    o_ref, run_ref, acc_ref = rest[2 * PAGES_PER_STEP:]
    g = pl.program_id(1)
    qidx = _sample_row_query(n_q)

    def chunk(kvh, q, k, v, past, u):
        sub = u.shape[0]
        z = _dot_nt(q, k) * ATT_SCALE
        ls, lnk = _log_sigmoid_pair(z)
        n_sub = k.shape[0] // sub
        for sb in range(n_sub - 1, -1, -1):
            cols = slice(sb * sub, (sb + 1) * sub)
            lk = lnk[:, cols] if past is None else jnp.where(past[:, cols], lnk[:, cols], 0.0)
            between = _suffix_exclusive(lk, u) + run_ref[kvh]
            w = jnp.exp(ls[:, cols] + between)
            if past is not None:
                w = jnp.where(past[:, cols], w, 0.0)
            acc_ref[kvh] += _dot(w.astype(BF16), v[cols, :])
            run_ref[kvh] += jnp.sum(lk, axis=1, keepdims=True)

    @pl.when(g == 0)
    def _():
        run_ref[...] = jnp.zeros_like(run_ref)
        acc_ref[...] = jnp.zeros_like(acc_ref)
        lane = lax.broadcasted_iota(jnp.int32, (1, PAGE_SIZE), 1)
        past = lane < qidx
        u_page = u_ref[:PAGE_SIZE, :PAGE_SIZE]
        for kvh in range(N_KV_HEADS):
            q = q_ref[kvh].astype(BF16)
            chunk(kvh, q, _kv_page_head(kn_ref, kvh), _kv_page_head(vn_ref, kvh), past, u_page)

    for kvh in range(N_KV_HEADS):
        q = q_ref[kvh].astype(BF16)
        chunk(kvh, q, _gather_kv(k_pages, kvh), _gather_kv(v_pages, kvh), None, u_ref[...])

    @pl.when(g == pl.num_programs(1) - 1)
    def _():
        o_ref[...] = acc_ref[...].astype(o_ref.dtype)


def _softmax_sample_kernel(pt_ref, q_ref, kn_ref, vn_ref, *rest, n_q, kind, n_groups):
    if kind == "fox":
        fnew_ref = rest[0]
        f_pages = rest[1:1 + PAGES_PER_STEP]
        rest = rest[1 + PAGES_PER_STEP:]
    else:
        sel_ref = rest[0]
        rest = rest[1:]
    k_pages = rest[:PAGES_PER_STEP]
    v_pages = rest[PAGES_PER_STEP:2 * PAGES_PER_STEP]
    o_ref, m_ref, l_ref, acc_ref = rest[2 * PAGES_PER_STEP:2 * PAGES_PER_STEP + 4]
    fcarry_ref = rest[2 * PAGES_PER_STEP + 4] if kind == "fox" else None
    g = pl.program_id(1)
    rows = n_q * KV_GROUP
    qidx = _sample_row_query(n_q)
    chunk_keys = PAGES_PER_STEP * PAGE_SIZE
    past_len = n_groups * chunk_keys
    first_slot = (n_groups - 1 - g) * PAGES_PER_STEP

    @pl.when(g == 0)
    def _():
        m_ref[...] = jnp.full_like(m_ref, MASKED)
        l_ref[...] = jnp.zeros_like(l_ref)
        acc_ref[...] = jnp.zeros_like(acc_ref)
        lane = lax.broadcasted_iota(jnp.int32, (1, PAGE_SIZE), 1)
        valid = lane <= qidx
        if kind == "fox":
            fcarry_ref[...] = jnp.zeros_like(fcarry_ref)
        for kvh in range(N_KV_HEADS):
            q = q_ref[kvh].astype(BF16)
            s = _dot_nt(q, _kv_page_head(kn_ref, kvh)) * ATT_SCALE
            if kind == "fox":
                f = fnew_ref[kvh]
                bias = jnp.zeros((rows, PAGE_SIZE), F32)
                for j in range(n_q):
                    between = (lane > j) & (lane <= qidx)
                    bj = jnp.sum(jnp.where(between, f, 0.0), axis=1, keepdims=True)
                    bias = jnp.where(lane == j, bj, bias)
                s = s + bias
            else:
                slope = _alibi_slopes_rows_sample(kvh, n_q)
                s = s - slope * (qidx - lane).astype(F32)
            s = jnp.where(valid, s, MASKED)
            _softmax_update(s, valid, _kv_page_head(vn_ref, kvh), m_ref.at[kvh], l_ref.at[kvh], acc_ref.at[kvh])

    if kind == "fox":
        f = jnp.concatenate([r[...] for r in f_pages], axis=1)
        lane_c = lax.broadcasted_iota(jnp.int32, f.shape, 1)
        incl = f
        d = 1
        while d < chunk_keys:
            shifted = pltpu.roll(incl, chunk_keys - d, axis=1)
            incl = incl + jnp.where(lane_c + d < chunk_keys, shifted, 0.0)
            d *= 2
        suffix = incl - f + fcarry_ref[...]
        lane = lax.broadcasted_iota(jnp.int32, (1, PAGE_SIZE), 1)

    for kvh in range(N_KV_HEADS):
        q = q_ref[kvh].astype(BF16)
        s = _dot_nt(q, _gather_kv(k_pages, kvh)) * ATT_SCALE
        if kind == "fox":
            f_new = fnew_ref[kvh]
            upto_q = jnp.sum(jnp.where(lane <= qidx, f_new, 0.0), axis=1, keepdims=True)
            sfx = suffix[kvh * KV_GROUP:(kvh + 1) * KV_GROUP, :]
            s = s + jnp.concatenate([sfx] * n_q, axis=0) + upto_q
            valid = None
        else:
            slope = _alibi_slopes_rows_sample(kvh, n_q)
            kpos = first_slot * PAGE_SIZE + lax.broadcasted_iota(jnp.int32, (1, chunk_keys), 1)
            s = s - slope * (past_len + qidx - kpos).astype(F32)
            sel = sel_ref[kvh]
            sel_lane = lax.broadcasted_iota(jnp.int32, sel.shape, 1)
            pages_per_block = MOBA_BLOCK // PAGE_SIZE
            cols = []
            for bb in range(PAGES_PER_STEP // pages_per_block):
                blk = first_slot // pages_per_block + bb
                picked = jnp.sum(jnp.where(sel_lane == blk, sel, 0.0), axis=1, keepdims=True) > 0.0
                cols.append(jnp.broadcast_to(picked, (rows, MOBA_BLOCK)))
            valid = jnp.concatenate(cols, axis=1)
            s = jnp.where(valid, s, MASKED)
        _softmax_update(s, valid, _gather_kv(v_pages, kvh), m_ref.at[kvh], l_ref.at[kvh], acc_ref.at[kvh])

    if kind == "fox":
        fcarry_ref[...] += jnp.sum(f, axis=1, keepdims=True)

    @pl.when(g == pl.num_programs(1) - 1)
    def _():
        o_ref[...] = (acc_ref[...] / l_ref[...]).astype(o_ref.dtype)


def _alibi_slopes_rows_sample(kvh, n_q):
    r = lax.broadcasted_iota(jnp.int32, (n_q * KV_GROUP, 1), 0)
    h = kvh * KV_GROUP + r % KV_GROUP
    return jnp.exp2(-8.0 * (h + 1).astype(F32) / N_HEADS)


def _moba_sample_select_kernel(pt_ref, q_ref, *rest, n_groups):
    k_pages = rest[:PAGES_PER_STEP]
    sel_ref, kmean_ref = rest[PAGES_PER_STEP:]
    g = pl.program_id(1)
    pages_per_block = MOBA_BLOCK // PAGE_SIZE
    blocks_per_step = PAGES_PER_STEP // pages_per_block

    @pl.when(g == 0)
    def _():
        kmean_ref[...] = jnp.zeros_like(kmean_ref)

    first_block = pl.multiple_of((n_groups - 1 - g) * blocks_per_step, blocks_per_step)
    for kvh in range(N_KV_HEADS):
        means = []
        for bb in range(blocks_per_step):
            tot = None
            for p in range(pages_per_block):
                page = k_pages[bb * pages_per_block + p][pl.ds(kvh, PAGE_SIZE, stride=N_KV_HEADS), :]
                part = jnp.sum(page, axis=0, keepdims=True)
                tot = part if tot is None else tot + part
            means.append(tot * (1.0 / MOBA_BLOCK))
        kmean_ref[kvh, pl.ds(first_block, blocks_per_step), :] = jnp.concatenate(means, axis=0)

    @pl.when(g == n_groups - 1)
    def _():
        n_blocks = n_groups * blocks_per_step
        for kvh in range(N_KV_HEADS):
            gate = _dot_nt_precise(q_ref[kvh], kmean_ref[kvh])
            sel_ref[kvh] = _moba_select(gate, n_blocks, n_blocks)


def _sample_call(kern, name, page_table, n_seq, n_pages, fixed_ins, fixed_specs, paged_ins, paged_specs,
                 out_shape, out_spec, scratch):
    n_groups = n_pages // PAGES_PER_STEP
    ins = list(fixed_ins)
    specs = list(fixed_specs)
    for arr, sp in zip(paged_ins, paged_specs):
        ins += [arr] * PAGES_PER_STEP
        specs += sp
    return pl.pallas_call(
        kern,
        grid_spec=pltpu.PrefetchScalarGridSpec(
            num_scalar_prefetch=1, grid=(n_seq, n_groups), in_specs=specs, out_specs=out_spec,
            scratch_shapes=scratch),
        out_shape=out_shape,
        compiler_params=_cparams(("arbitrary", "arbitrary")), name=name,
    )(page_table, *ins)


def _seq_spec(shape):
    nd = len(shape)
    return pl.BlockSpec((None,) + shape, lambda s, g, pt: (s,) + (0,) * nd)


def _const_spec(shape):
    nd = len(shape)
    return pl.BlockSpec(shape, lambda s, g, pt: (0,) * nd)


def _gqa_sample_attention(kind, page_table, q_rows, k_new_page, v_new_page, cache_k, cache_v, layer,
                          extra=None):
    n_seq, _, rows, _ = q_rows.shape
    n_q = rows // KV_GROUP
    n_pages = page_table.shape[0] // n_seq
    n_groups = n_pages // PAGES_PER_STEP
    kv_block = (PAGE_SIZE * N_KV_HEADS, HEAD_DIM)
    kv_specs = _page_specs(kv_block, layer, n_pages, n_groups)
    q_spec = _seq_spec((N_KV_HEADS, rows, HEAD_DIM))
    new_spec = _seq_spec(kv_block)
    out_shape = jax.ShapeDtypeStruct((n_seq, N_KV_HEADS, rows, HEAD_DIM), BF16)
    col = pltpu.VMEM((N_KV_HEADS, rows, 1), F32)
    acc = pltpu.VMEM((N_KV_HEADS, rows, HEAD_DIM), F32)
    if kind == "sb":
        (u,) = extra
        return _sample_call(
            functools.partial(_sb_sample_kernel, n_q=n_q), "sb_sample", page_table, n_seq, n_pages,
            [q_rows, k_new_page, v_new_page, u], [q_spec, new_spec, new_spec, _const_spec(u.shape)],
            [cache_k, cache_v], [kv_specs, kv_specs], out_shape, q_spec, [col, acc])
    kern = functools.partial(_softmax_sample_kernel, n_q=n_q, kind=kind, n_groups=n_groups)
    if kind == "fox":
        f_new_rows, cache_logf_t = extra
        f_specs = _page_specs((N_HEADS, PAGE_SIZE), layer, n_pages, n_groups)
        return _sample_call(
            kern, "fox_sample", page_table, n_seq, n_pages,
            [q_rows, k_new_page, v_new_page, f_new_rows],
            [q_spec, new_spec, new_spec, _seq_spec((N_KV_HEADS, rows, LANES))],
            [cache_logf_t, cache_k, cache_v], [f_specs, kv_specs, kv_specs], out_shape, q_spec,
            [col, col, acc, pltpu.VMEM((N_HEADS, 1), F32)])
    sel = _sample_call(
        functools.partial(_moba_sample_select_kernel, n_groups=n_groups), "moba_sample_select",
        page_table, n_seq, n_pages, [q_rows], [q_spec], [cache_k], [kv_specs],
        jax.ShapeDtypeStruct((n_seq, N_KV_HEADS, rows, LANES), F32),
        _seq_spec((N_KV_HEADS, rows, LANES)), [pltpu.VMEM((N_KV_HEADS, LANES, HEAD_DIM), F32)])
    return _sample_call(
        kern, "moba_sample", page_table, n_seq, n_pages,
        [q_rows, k_new_page, v_new_page, sel],
        [q_spec, new_spec, new_spec, _seq_spec((N_KV_HEADS, rows, LANES))],
        [cache_k, cache_v], [kv_specs, kv_specs], out_shape, q_spec, [col, col, acc])


MLA_PAGES_PER_STEP = PAGES_PER_STEP


def _mla_sample_kernel(pt_ref, ql_ref, qr_ref, cn_ref, rn_ref, *rest, n_q):
    c_pages = rest[:PAGES_PER_STEP]
    r_pages = rest[PAGES_PER_STEP:2 * PAGES_PER_STEP]
    o_ref, m_ref, l_ref, acc_ref = rest[2 * PAGES_PER_STEP:]
    g = pl.program_id(1)
    ql = ql_ref[...]
    qr = qr_ref[...][:, :MLA_ROPE]
    r = lax.broadcasted_iota(jnp.int32, (ql.shape[0], 1), 0)
    qidx = r % n_q

    @pl.when(g == 0)
    def _():
        m_ref[...] = jnp.full_like(m_ref, MASKED)
        l_ref[...] = jnp.zeros_like(l_ref)
        acc_ref[...] = jnp.zeros_like(acc_ref)
        c = cn_ref[...].astype(BF16)
        s = (_dot_nt(ql, c) + _dot_nt(qr, rn_ref[...].astype(BF16))) * MLA_SCALE
        lane = lax.broadcasted_iota(jnp.int32, (1, PAGE_SIZE), 1)
        valid = lane <= qidx
        s = jnp.where(valid, s, MASKED)
        _softmax_update(s, valid, c, m_ref, l_ref, acc_ref)

    c = jnp.concatenate([p[...].astype(BF16) for p in c_pages], axis=0)
    kr = jnp.concatenate([p[...].astype(BF16) for p in r_pages], axis=0)
    s = (_dot_nt(ql, c) + _dot_nt(qr, kr)) * MLA_SCALE
    _softmax_update(s, None, c, m_ref, l_ref, acc_ref)

    @pl.when(g == pl.num_programs(1) - 1)
    def _():
        o_ref[...] = (acc_ref[...] / l_ref[...]).astype(o_ref.dtype)


def _mla_sample_attention(page_table, q_lat, q_pe, c_new_page, r_new_page, cache_ckv, cache_kpe, layer):
    n_seq, rows, rank = q_lat.shape
    n_q = rows // N_HEADS
    n_pages = page_table.shape[0] // n_seq
    n_groups = n_pages // PAGES_PER_STEP
    c_block = (PAGE_SIZE, rank)
    r_block = (PAGE_SIZE, MLA_ROPE)
    return _sample_call(
        functools.partial(_mla_sample_kernel, n_q=n_q), "mla_sample", page_table, n_seq, n_pages,
        [q_lat, q_pe, c_new_page, r_new_page],
        [_seq_spec((rows, rank)), _seq_spec((rows, LANES)), _seq_spec(c_block), _seq_spec(r_block)],
        [cache_ckv, cache_kpe],
        [_page_specs(c_block, layer, n_pages, n_groups), _page_specs(r_block, layer, n_pages, n_groups)],
        jax.ShapeDtypeStruct((n_seq, rows, rank), BF16), _seq_spec((rows, rank)),
        [pltpu.VMEM((rows, 1), F32), pltpu.VMEM((rows, 1), F32), pltpu.VMEM((rows, rank), F32)])


def _cumsum_lanes_kernel(x_ref, o_ref):
    x = x_ref[...]
    n = x.shape[1]
    lane = lax.broadcasted_iota(jnp.int32, x.shape, 1)
    d = 1
    while d < n:
        x = x + jnp.where(lane >= d, pltpu.roll(x, d, axis=1), 0.0)
        d *= 2
    o_ref[...] = x


def _cumsum_lanes(x):
    b, r, t = x.shape
    return pl.pallas_call(
        _cumsum_lanes_kernel, grid=(b,),
        in_specs=[pl.BlockSpec((None, r, t), lambda i: (i, 0, 0))],
        out_specs=pl.BlockSpec((None, r, t), lambda i: (i, 0, 0)),
        out_shape=jax.ShapeDtypeStruct(x.shape, F32),
        compiler_params=_cparams(("arbitrary",)), name="cumsum_lanes",
    )(x)


def _sample_rows(x, n_seq, n_q):
    x = x.reshape(n_seq, n_q, N_KV_HEADS, KV_GROUP, HEAD_DIM)
    return jnp.transpose(x, (0, 2, 1, 3, 4)).reshape(n_seq, N_KV_HEADS, n_q * KV_GROUP, HEAD_DIM)


def _sample_rows_back(o, n_seq, n_q):
    o = o.reshape(n_seq, N_KV_HEADS, n_q, KV_GROUP, HEAD_DIM)
    return jnp.transpose(o, (0, 2, 1, 3, 4)).reshape(n_seq * n_q, N_HEADS * HEAD_DIM)


def _new_page(rows, n_seq, n_q):
    w = rows.shape[1]
    per_tok = w // HEAD_DIM
    page = rows.reshape(n_seq, n_q * per_tok, HEAD_DIM)
    return jnp.pad(page, ((0, 0), (0, (PAGE_SIZE - n_q) * per_tok), (0, 0)))


def _suffix_matrix(n):
    j = lax.broadcasted_iota(jnp.int32, (n, n), 0)
    s = lax.broadcasted_iota(jnp.int32, (n, n), 1)
    return (j > s).astype(BF16)


def _rope_tables(pos):
    half = MLA_ROPE // 2
    inv = ROPE_THETA ** (-jnp.arange(half, dtype=F32) / half)
    ang = pos.astype(F32)[:, None] * inv[None, :]
    zeros = jnp.zeros((pos.shape[0], LANES - MLA_ROPE), F32)
    cos, sin = jnp.cos(ang), jnp.sin(ang)
    return jnp.concatenate([cos, cos, zeros], 1), jnp.concatenate([sin, sin, zeros], 1)


def _rotate_half_weights(w):
    half = MLA_ROPE // 2
    x1, x2 = w[..., :half], w[..., half:]
    pad = jnp.zeros(w.shape[:-1] + (LANES - MLA_ROPE,), w.dtype)
    return jnp.concatenate([x1, x2, pad], -1), jnp.concatenate([-x2, x1, pad], -1)


def kernel(x_prompt, x_sample, cache_sb_k, cache_sb_v, cache_mla_ckv, cache_mla_kpe, cache_fox_k, cache_fox_v, cache_fox_logf, cache_moba_k, cache_moba_v, page_table, ln_g, ln_b, sb_w_qkv, sb_w_o, mla_w_dq, mla_q_norm, mla_w_uq, mla_w_dkv, mla_kv_norm, mla_w_ukv, mla_w_o, fox_w_qkv, fox_w_f, fox_b_f, fox_w_o, moba_w_qkv, moba_w_o, router_w, router_b, moe_w_gate, moe_w_up, moe_w_down):
    batch, seq, d_model = x_prompt.shape
    n_seq, n_q, _ = x_sample.shape
    n_pages = page_table.shape[1]
    past_len = n_pages * PAGE_SIZE
    mp = batch * seq
    ms = n_seq * n_q
    m = mp + ms
    nq_w = N_HEADS * HEAD_DIM
    nk_w = N_KV_HEADS * HEAD_DIM

    x = jnp.concatenate([x_prompt.reshape(mp, d_model), x_sample.reshape(ms, d_model)], axis=0)
    pt_flat = page_table.reshape(-1)
    u = _suffix_matrix(MOBA_BLOCK)
    rw_hi, rw_lo = _split_bf16(router_w.T)
    rb = router_b.reshape(N_EXPERTS, 1).astype(F32)
    pos = jnp.concatenate([jnp.tile(jnp.arange(seq, dtype=jnp.int32), batch),
                           jnp.tile(past_len + jnp.arange(n_q, dtype=jnp.int32), n_seq)])
    cos_t, sin_t = _rope_tables(pos)

    def kv_pages(cache):
        return cache.reshape(cache.shape[0], cache.shape[1], PAGE_SIZE * N_KV_HEADS, HEAD_DIM)

    def kv_outputs(k, v):
        kp = k[:mp].reshape(1, batch, seq, N_KV_HEADS, HEAD_DIM)
        vp = v[:mp].reshape(1, batch, seq, N_KV_HEADS, HEAD_DIM)
        ks = k[mp:].reshape(1, n_seq, n_q, N_KV_HEADS, HEAD_DIM)
        vs = v[mp:].reshape(1, n_seq, n_q, N_KV_HEADS, HEAD_DIM)
        return kp, vp, ks, vs

    def gqa_layer(kind, w_qkv, cache_k, cache_v, w_f=None, b_f=None, cache_logf=None):
        outs = [(nq_w, F32), (nk_w, F32), (nk_w, F32)]
        if kind == "fox":
            w = jnp.concatenate([w_qkv, jnp.pad(w_f, ((0, 0), (0, LANES - N_HEADS)))], axis=1).astype(BF16)
            bf = jnp.pad(b_f.astype(F32), (0, LANES - N_HEADS)).reshape(1, LANES)
            q, k, v, logf = _fused_matmul(x, w, _qkv_fox_epilogue, outs + [(LANES, F32)],
                                          const_ins=(bf,), name="qkv_fox")
            logf = logf[:, :N_HEADS]
        else:
            q, k, v = _fused_matmul(x, w_qkv.astype(BF16), _qkv_epilogue, outs, name="qkv_" + kind)
        q_rows = _sample_rows(q[mp:], n_seq, n_q)
        k_new = _new_page(k[mp:], n_seq, n_q)
        v_new = _new_page(v[mp:], n_seq, n_q)
        if kind == "sb":
            op = _gqa_prompt_attention("sb", q, k, v, batch, seq, (u,))
            o_s = _gqa_sample_attention("sb", pt_flat, q_rows, k_new, v_new, kv_pages(cache_k),
                                        kv_pages(cache_v), 0, (u,))
            state = kv_outputs(k, v)
        elif kind == "fox":
            lf_p = logf[:mp].reshape(batch, seq, N_HEADS)
            c_t = _cumsum_lanes(jnp.transpose(lf_p, (0, 2, 1)))
            c_t = c_t.reshape(batch, N_KV_HEADS, KV_GROUP, seq)
            c = jnp.transpose(c_t, (0, 1, 3, 2))
            op = _gqa_prompt_attention("fox", q, k, v, batch, seq, (c, c_t))
            lf_s = logf[mp:].reshape(n_seq, n_q, N_KV_HEADS, KV_GROUP)
            f_rows = jnp.transpose(lf_s, (0, 2, 3, 1))
            f_rows = jnp.broadcast_to(f_rows[:, :, None], (n_seq, N_KV_HEADS, n_q, KV_GROUP, n_q))
            f_rows = f_rows.reshape(n_seq, N_KV_HEADS, n_q * KV_GROUP, n_q)
            f_rows = jnp.pad(f_rows, ((0, 0), (0, 0), (0, 0), (0, LANES - n_q)))
            logf_t = jnp.transpose(cache_logf, (0, 1, 3, 2))
            o_s = _gqa_sample_attention("fox", pt_flat, q_rows, k_new, v_new, kv_pages(cache_k),
                                        kv_pages(cache_v), 0, (f_rows, logf_t))
            kp, vp, ks, vs = kv_outputs(k, v)
            state = (kp, vp, lf_p[None], ks, vs, logf[mp:].reshape(1, n_seq, n_q, N_HEADS))
        else:
            op = _gqa_prompt_attention("moba", q, k, v, batch, seq)
            o_s = _gqa_sample_attention("moba", pt_flat, q_rows, k_new, v_new, kv_pages(cache_k),
                                        kv_pages(cache_v), 0)
            state = kv_outputs(k, v)
        o = jnp.concatenate([op, _sample_rows_back(o_s, n_seq, n_q)], axis=0)
        return o, state

    def mla_layer(w_dq, q_norm, w_uq, w_dkv, kv_norm, w_ukv, cache_ckv, cache_kpe):
        q_rank = w_dq.shape[1]
        kv_rank = kv_norm.shape[0]
        ka, kb = _rotate_half_weights(w_dkv[:, kv_rank:])
        w_down = jnp.concatenate([w_dq, w_dkv[:, :kv_rank], ka, kb], axis=1).astype(BF16)
        cq, ckv, kpe = _fused_matmul(
            x, w_down, _mla_down_epilogue, [(q_rank, BF16), (kv_rank, F32), (LANES, F32)],
            row_ins=(cos_t, sin_t), const_ins=(q_norm.reshape(1, -1), kv_norm.reshape(1, -1)), name="mla_down")
        w_uq3 = w_uq.reshape(q_rank, N_HEADS, MLA_NOPE + MLA_ROPE)
        qa, qb = _rotate_half_weights(w_uq3[..., MLA_NOPE:])
        w_q = jnp.concatenate([w_uq3[..., :MLA_NOPE].reshape(q_rank, -1), qa.reshape(q_rank, -1),
                               qb.reshape(q_rank, -1)], axis=1).astype(BF16)
        q_nope, q_pe = _fused_matmul(cq, w_q, _mla_q_epilogue, [(nq_w, BF16), (nq_w, BF16)],
                                     row_ins=(cos_t, sin_t), name="mla_q")
        kv_up, = _fused_matmul(ckv[:mp], w_ukv.astype(BF16), _plain_epilogue,
                               [(w_ukv.shape[1], BF16)], name="mla_kv_up")
        op = _mla_prompt_attention(q_nope, q_pe, kv_up, kpe, batch, seq)
        w_ukv3 = w_ukv.reshape(kv_rank, N_HEADS, MLA_NOPE + MLA_V)
        w_uk_t = jnp.transpose(w_ukv3[..., :MLA_NOPE], (1, 2, 0))
        w_uv = jnp.transpose(w_ukv3[..., MLA_NOPE:], (1, 0, 2))
        qn_s = jnp.transpose(q_nope[mp:].reshape(ms, N_HEADS, MLA_NOPE), (1, 0, 2))
        q_lat = _batched_matmul(qn_s, w_uk_t, BF16, "mla_q_lat")
        q_lat = jnp.transpose(q_lat.reshape(N_HEADS, n_seq, n_q, kv_rank), (1, 0, 2, 3))
        q_lat = q_lat.reshape(n_seq, N_HEADS * n_q, kv_rank)
        qr_s = jnp.transpose(q_pe[mp:].reshape(n_seq, n_q, N_HEADS, HEAD_DIM), (0, 2, 1, 3))
        qr_s = qr_s.reshape(n_seq, N_HEADS * n_q, HEAD_DIM)
        c_new = jnp.pad(ckv[mp:].reshape(n_seq, n_q, kv_rank), ((0, 0), (0, PAGE_SIZE - n_q), (0, 0)))
        r_new = jnp.pad(kpe[mp:, :MLA_ROPE].reshape(n_seq, n_q, MLA_ROPE), ((0, 0), (0, PAGE_SIZE - n_q), (0, 0)))
        o_lat = _mla_sample_attention(pt_flat, q_lat, qr_s, c_new, r_new, cache_ckv, cache_kpe, 0)
        o_lat = jnp.transpose(o_lat.reshape(n_seq, N_HEADS, n_q, kv_rank), (1, 0, 2, 3))
        o_s = _batched_matmul(o_lat.reshape(N_HEADS, ms, kv_rank), w_uv, BF16, "mla_o_up")
        o_s = jnp.transpose(o_s, (1, 0, 2)).reshape(ms, N_HEADS * MLA_V)
        o = jnp.concatenate([op, o_s], axis=0)
        state = (ckv[:mp].reshape(1, batch, seq, kv_rank), kpe[:mp, :MLA_ROPE].reshape(1, batch, seq, MLA_ROPE),
                 ckv[mp:].reshape(1, n_seq, n_q, kv_rank), kpe[mp:, :MLA_ROPE].reshape(1, n_seq, n_q, MLA_ROPE))
        return o, state

    moe_tm = m // 16
    states = {}
    for layer in range(DEPTH):
        kind = layer % 4
        if kind == 0:
            o, states["sb"] = gqa_layer("sb", sb_w_qkv[0], cache_sb_k, cache_sb_v)
            w_o = sb_w_o[0]
        elif kind == 1:
            o, states["mla"] = mla_layer(mla_w_dq[0], mla_q_norm[0], mla_w_uq[0], mla_w_dkv[0], mla_kv_norm[0],
                                         mla_w_ukv[0], cache_mla_ckv, cache_mla_kpe)
            w_o = mla_w_o[0]
        elif kind == 2:
            o, states["fox"] = gqa_layer("fox", fox_w_qkv[0], cache_fox_k, cache_fox_v, fox_w_f[0], fox_b_f[0],
                                         cache_fox_logf)
            w_o = fox_w_o[0]
        else:
            o, states["moba"] = gqa_layer("moba", moba_w_qkv[0], cache_moba_k, cache_moba_v)
            w_o = moba_w_o[0]
        x1, gates_t = _oproj_norm_route(o, w_o.astype(BF16), x, ln_g[layer, 0].reshape(1, -1),
                                        ln_b[layer, 0].reshape(1, -1), rw_hi, rw_lo, rb)
        x = _moe_norm(x1, gates_t.T, moe_w_gate[layer].astype(BF16), moe_w_up[layer].astype(BF16),
                      moe_w_down[layer].astype(BF16), ln_g[layer, 1].reshape(1, -1),
                      ln_b[layer, 1].reshape(1, -1), moe_tm)

    y_prompt = x[:mp].reshape(batch, seq, d_model)
    y_sample = x[mp:].reshape(n_seq, n_q, d_model)
    return (y_prompt, y_sample) + states["sb"] + states["mla"] + states["fox"] + states["moba"]
```

```python
import functools

import jax
import jax.numpy as jnp
from jax import lax
from jax.experimental import pallas as pl
from jax.experimental.pallas import tpu as pltpu

F32 = jnp.float32
BF16 = jnp.bfloat16

HEAD_DIM = 128
N_HEADS = 16
N_KV_HEADS = 2
KV_GROUP = N_HEADS // N_KV_HEADS
PAGE_SIZE = 128
MLA_NOPE = 128
MLA_ROPE = 64
MLA_V = 128
MLA_SCALE = (MLA_NOPE + MLA_ROPE) ** -0.5
ROPE_THETA = 10000.0
MOBA_BLOCK = 256
MOBA_TOPK = 3
N_EXPERTS = 16
EXPERTS_PER_GROUP = 4
N_GROUPS = N_EXPERTS // EXPERTS_PER_GROUP
DEPTH = 4
DN_ALPHA = (2 * DEPTH) ** 0.25
LN_EPS = 1e-5
RMS_EPS = 1e-6
ATT_SCALE = HEAD_DIM ** -0.5

LANES = 128
SUBLANES = 8
MASKED = -1e30
EXP_UNDERFLOW = -104.0
VMEM_LIMIT_BYTES = 56 * 1024 * 1024

NT_DIMS = (((1,), (1,)), ((), ()))


def _cparams(semantics):
    return pltpu.CompilerParams(dimension_semantics=semantics, vmem_limit_bytes=VMEM_LIMIT_BYTES)


def _dot(a, b):
    return jnp.dot(a, b, preferred_element_type=F32)


def _dot_nt(a, b):
    return lax.dot_general(a, b, NT_DIMS, preferred_element_type=F32)


def _split_bf16(x):
    hi = x.astype(BF16)
    lo = (x - hi.astype(F32)).astype(BF16)
    return hi, lo


def _dot_nt_precise(a, b):
    a_hi, a_lo = _split_bf16(a)
    b_hi, b_lo = _split_bf16(b)
    return _dot_nt(a_hi, b_hi) + _dot_nt(a_lo, b_hi) + _dot_nt(a_hi, b_lo)


def _log_sigmoid_pair(z):
    sp = jnp.log(1.0 + jnp.exp(-jnp.abs(z)))
    return jnp.minimum(z, 0.0) - sp, -jnp.maximum(z, 0.0) - sp


def _suffix_exclusive(x, u):
    hi, lo = _split_bf16(x)
    return _dot(hi, u) + _dot(lo, u)


def _layer_norm(y, g, b):
    mean = jnp.mean(y, axis=-1, keepdims=True)
    yc = y - mean
    var = jnp.mean(yc * yc, axis=-1, keepdims=True)
    return yc * lax.rsqrt(var + LN_EPS) * g + b


def _rms_norm(y, g):
    return y * lax.rsqrt(jnp.mean(y * y, axis=-1, keepdims=True) + RMS_EPS) * g


def _softmax_update(s, valid, v, m_ref, l_ref, acc_ref):
    m_old = m_ref[...]
    m_new = jnp.maximum(m_old, jnp.max(s, axis=1, keepdims=True))
    alpha = jnp.exp(m_old - m_new)
    p = jnp.exp(s - m_new)
    if valid is not None:
        p = jnp.where(valid, p, 0.0)
    l_ref[...] = alpha * l_ref[...] + jnp.sum(p, axis=1, keepdims=True)
    acc_ref[...] = alpha * acc_ref[...] + _dot(p.astype(BF16), v)
    m_ref[...] = m_new


def _fused_matmul_kernel(*refs, n_row, n_const, epilogue):
    x_ref, w_ref = refs[0], refs[1]
    row_refs = refs[2:2 + n_row]
    const_refs = refs[2 + n_row:2 + n_row + n_const]
    out_refs = refs[2 + n_row + n_const:]
    acc = _dot(x_ref[...].astype(BF16), w_ref[...])
    outs = epilogue(acc, [r[...] for r in row_refs], [c[...] for c in const_refs])
    for o_ref, val in zip(out_refs, outs):
        o_ref[...] = val.astype(o_ref.dtype)


def _fused_matmul(x, w, epilogue, outs, row_ins=(), const_ins=(), t_outs=(), tm=256, name="fused_matmul"):
    m, k = x.shape
    n = w.shape[1]
    assert m % tm == 0 and w.shape[0] == k
    in_specs = [pl.BlockSpec((tm, k), lambda i: (i, 0)),
                pl.BlockSpec((k, n), lambda i: (0, 0), pipeline_mode=pl.Buffered(1))]
    in_specs += [pl.BlockSpec((tm, a.shape[1]), lambda i: (i, 0)) for a in row_ins]
    in_specs += [pl.BlockSpec(c.shape, lambda i: (0, 0)) for c in const_ins]
    out_shape = [jax.ShapeDtypeStruct((m, width), dt) for width, dt in outs]
    out_specs = [pl.BlockSpec((tm, width), lambda i: (i, 0)) for width, _ in outs]
    out_shape += [jax.ShapeDtypeStruct((rows, m), dt) for rows, dt in t_outs]
    out_specs += [pl.BlockSpec((rows, tm), lambda i: (0, i)) for rows, _ in t_outs]
    return pl.pallas_call(
        functools.partial(_fused_matmul_kernel, n_row=len(row_ins), n_const=len(const_ins),
                          epilogue=epilogue),
        grid=(m // tm,), in_specs=in_specs, out_specs=out_specs, out_shape=out_shape,
        compiler_params=_cparams(("arbitrary",)), name=name,
    )(x, w, *row_ins, *const_ins)


def _bmm_kernel(x_ref, w_ref, o_ref):
    o_ref[...] = _dot(x_ref[...].astype(BF16), w_ref[...].astype(BF16)).astype(o_ref.dtype)


def _batched_matmul(x, w, out_dtype, name):
    h, m, k = x.shape
    n = w.shape[2]
    return pl.pallas_call(
        _bmm_kernel, grid=(h,),
        in_specs=[pl.BlockSpec((None, m, k), lambda i: (i, 0, 0)),
                  pl.BlockSpec((None, k, n), lambda i: (i, 0, 0))],
        out_specs=pl.BlockSpec((None, m, n), lambda i: (i, 0, 0)),
        out_shape=jax.ShapeDtypeStruct((h, m, n), out_dtype),
        compiler_params=_cparams(("arbitrary",)), name=name,
    )(x, w)


def _qkv_epilogue(acc, rows, consts):
    nq = N_HEADS * HEAD_DIM
    nk = N_KV_HEADS * HEAD_DIM
    k = acc[:, nq:nq + nk]
    v = acc[:, nq + nk:nq + 2 * nk]
    return acc[:, :nq], k, v, k, v.T


def _qkv_fox_epilogue(acc, rows, consts):
    nq = N_HEADS * HEAD_DIM
    nk = N_KV_HEADS * HEAD_DIM
    (b_f,) = consts
    logf, _ = _log_sigmoid_pair(acc[:, nq + 2 * nk:] + b_f)
    k = acc[:, nq:nq + nk]
    v = acc[:, nq + nk:nq + 2 * nk]
    return acc[:, :nq], k, v, logf, k, v.T


def _mla_down_epilogue(acc, rows, consts):
    cos, sin = rows
    q_norm, kv_norm = consts
    r = q_norm.shape[1]
    c = kv_norm.shape[1]
    cq = _rms_norm(acc[:, :r], q_norm)
    ckv = _rms_norm(acc[:, r:r + c], kv_norm)
    kpe = acc[:, r + c:r + c + LANES] * cos + acc[:, r + c + LANES:] * sin
    return cq, ckv, kpe


def _mla_q_epilogue(acc, rows, consts):
    cos, sin = rows
    n = N_HEADS * HEAD_DIM
    cos_t = jnp.concatenate([cos] * N_HEADS, axis=1)
    sin_t = jnp.concatenate([sin] * N_HEADS, axis=1)
    return acc[:, :n], acc[:, n:2 * n] * cos_t + acc[:, 2 * n:] * sin_t


def _route(x1, rw_hi, rw_lo, rb):
    x_hi, x_lo = _split_bf16(x1)
    logits = _dot_nt(rw_hi, x_hi) + _dot_nt(rw_hi, x_lo) + _dot_nt(rw_lo, x_hi)
    aff = 1.0 / (1.0 + jnp.exp(-logits))
    choice = aff + rb
    c = [choice[e:e + 1, :] for e in range(N_EXPERTS)]
    a = [aff[e:e + 1, :] for e in range(N_EXPERTS)]
    scores = []
    for g in range(N_GROUPS):
        cg = c[g * EXPERTS_PER_GROUP:(g + 1) * EXPERTS_PER_GROUP]
        best = None
        for i in range(EXPERTS_PER_GROUP):
            for j in range(i + 1, EXPERTS_PER_GROUP):
                pair = cg[i] + cg[j]
                best = pair if best is None else jnp.maximum(best, pair)
        scores.append(best)
    grp = jnp.zeros_like(scores[0], dtype=jnp.int32)
    best = scores[0]
    for g in range(1, N_GROUPS):
        better = scores[g] > best
        grp = jnp.where(better, g, grp)
        best = jnp.where(better, scores[g], best)
    w = []
    for e in range(N_EXPERTS):
        g = e // EXPERTS_PER_GROUP
        rank = jnp.zeros_like(grp)
        for e2 in range(g * EXPERTS_PER_GROUP, (g + 1) * EXPERTS_PER_GROUP):
            if e2 == e:
                continue
            beats = (c[e2] >= c[e]) if e2 < e else (c[e2] > c[e])
            rank = rank + beats.astype(jnp.int32)
        selected = (grp == g) & (rank < 2)
        w.append(jnp.where(selected, a[e], 0.0))
    denom = w[0]
    for e in range(1, N_EXPERTS):
        denom = denom + w[e]
    return jnp.concatenate([we / denom for we in w], axis=0)


def _oproj_norm_route_kernel(o_ref, w_ref, x_ref, g_ref, b_ref, rwh_ref, rwl_ref, rb_ref,
                             x1_ref, gates_ref):
    h = _dot(o_ref[...].astype(BF16), w_ref[...])
    x1 = _layer_norm(DN_ALPHA * x_ref[...] + h, g_ref[...], b_ref[...])
    x1_ref[...] = x1
    gates_ref[...] = _route(x1, rwh_ref[...], rwl_ref[...], rb_ref[...])


def _oproj_norm_route(o, w_o, x, ln_g, ln_b, rw_hi, rw_lo, rb, tm=256):
    m, d = x.shape
    k = o.shape[1]
    const = lambda shape: pl.BlockSpec(shape, lambda i: (0, 0))
    return pl.pallas_call(
        _oproj_norm_route_kernel, grid=(m // tm,),
        in_specs=[pl.BlockSpec((tm, k), lambda i: (i, 0)),
                  pl.BlockSpec((k, d), lambda i: (0, 0), pipeline_mode=pl.Buffered(1)),
                  pl.BlockSpec((tm, d), lambda i: (i, 0)),
                  const((1, d)), const((1, d)), const(rw_hi.shape), const(rw_lo.shape), const(rb.shape)],
        out_specs=[pl.BlockSpec((tm, d), lambda i: (i, 0)),
                   pl.BlockSpec((N_EXPERTS, tm), lambda i: (0, i))],
        out_shape=[jax.ShapeDtypeStruct((m, d), F32), jax.ShapeDtypeStruct((N_EXPERTS, m), F32)],
        compiler_params=_cparams(("arbitrary",)), name="oproj_norm_route",
    )(o, w_o, x, ln_g, ln_b, rw_hi, rw_lo, rb)


def _moe_kernel(x_ref, gates_ref, wg_ref, wu_ref, wd_ref, g_ref, b_ref, o_ref, xb_ref, acc_ref):
    e = pl.program_id(1)

    @pl.when(e == 0)
    def _():
        xb_ref[...] = x_ref[...].astype(BF16)
        acc_ref[...] = jnp.zeros_like(acc_ref)

    xb = xb_ref[...]
    gates = gates_ref[...]
    lane = lax.broadcasted_iota(jnp.int32, gates.shape, 1)
    gate = jnp.sum(jnp.where(lane == e, gates, 0.0), axis=1, keepdims=True)
    hg = _dot(xb, wg_ref[...])
    hu = _dot(xb, wu_ref[...])
    h = hg / (1.0 + jnp.exp(-hg)) * hu * gate
    acc_ref[...] += _dot(h.astype(BF16), wd_ref[...])

    @pl.when(e == N_EXPERTS - 1)
    def _():
        o_ref[...] = _layer_norm(DN_ALPHA * x_ref[...] + acc_ref[...], g_ref[...], b_ref[...])


def _moe_norm(x1, gates, w_gate, w_up, w_down, ln_g, ln_b, tm):
    m, d = x1.shape
    f = w_gate.shape[2]
    return pl.pallas_call(
        _moe_kernel, grid=(m // tm, N_EXPERTS),
        in_specs=[pl.BlockSpec((tm, d), lambda i, e: (i, 0)),
                  pl.BlockSpec((tm, N_EXPERTS), lambda i, e: (i, 0)),
                  pl.BlockSpec((None, d, f), lambda i, e: (e, 0, 0)),
                  pl.BlockSpec((None, d, f), lambda i, e: (e, 0, 0)),
                  pl.BlockSpec((None, f, d), lambda i, e: (e, 0, 0)),
                  pl.BlockSpec((1, d), lambda i, e: (0, 0)),
                  pl.BlockSpec((1, d), lambda i, e: (0, 0))],
        out_specs=pl.BlockSpec((tm, d), lambda i, e: (i, 0)),
        out_shape=jax.ShapeDtypeStruct((m, d), F32),
        scratch_shapes=[pltpu.VMEM((tm, d), BF16), pltpu.VMEM((tm, d), F32)],
        compiler_params=_cparams(("arbitrary", "arbitrary")), name="moe_norm",
    )(x1, gates, w_gate, w_up, w_down, ln_g, ln_b)


PROMPT_TQ = 128
PROMPT_TK = 256
SB_TK = 128
MLA_HEADS_PER_STEP = 8


def _key_positions(j, tk):
    return j * tk + lax.broadcasted_iota(jnp.int32, (tk, 1), 0)


def _query_positions(qi, tq):
    return qi * tq + lax.broadcasted_iota(jnp.int32, (1, tq), 1)


def _head_cols(h):
    return slice(h * HEAD_DIM, (h + 1) * HEAD_DIM)


HEADS_ISSUED_AHEAD = 8


def _issue_ahead(n, produce):
    queue = [produce(h) for h in range(min(HEADS_ISSUED_AHEAD, n))]
    for h in range(n):
        if h + HEADS_ISSUED_AHEAD < n:
            queue.append(produce(h + HEADS_ISSUED_AHEAD))
        yield h, queue.pop(0)


def _softmax_update_t(s, vt, h, m_ref, l_ref, acc_ref):
    m_old = m_ref[h:h + 1, :]
    m_new = jnp.maximum(m_old, jnp.max(s, axis=0, keepdims=True))
    alpha = jnp.exp(m_old - m_new)
    p = jnp.exp(s - m_new)
    l_ref[h:h + 1, :] = alpha * l_ref[h:h + 1, :] + jnp.sum(p, axis=0, keepdims=True)
    acc_ref[h] = alpha * acc_ref[h] + _dot(vt, p.astype(BF16))
    m_ref[h:h + 1, :] = m_new


def _init_softmax_state(m_ref, l_ref, acc_ref):
    m_ref[...] = jnp.full_like(m_ref, MASKED)
    l_ref[...] = jnp.zeros_like(l_ref)
    acc_ref[...] = jnp.zeros_like(acc_ref)


def _store_heads_t(o_ref, acc_ref, l_ref, n_heads):
    for h in range(n_heads):
        o = acc_ref[h] if l_ref is None else acc_ref[h] / l_ref[h:h + 1, :]
        o_ref[:, _head_cols(h)] = o.T.astype(o_ref.dtype)


def _sb_prompt_kernel(q_ref, k_ref, vt_ref, ut_ref, o_ref, qb_ref, run_ref, acc_ref, *, tq, tk):
    qi = pl.program_id(2)
    qb_ref[...] = q_ref[...].astype(BF16)
    run_ref[...] = jnp.zeros_like(run_ref)
    acc_ref[...] = jnp.zeros_like(acc_ref)
    tpos = _query_positions(qi, tq)
    jd = (qi * tq) // tk

    def chunk(j, diagonal):
        start = pl.multiple_of(j * tk, tk)
        k = k_ref[pl.ds(start, tk), :]
        vt = vt_ref[:, pl.ds(start, tk)]
        ut = ut_ref[...]
        if diagonal:
            past = _key_positions(j, tk) < tpos

        def gates(h):
            z = _dot_nt(k, qb_ref[:, _head_cols(h)]) * ATT_SCALE
            ls, lk = _log_sigmoid_pair(z)
            if diagonal:
                lk = jnp.where(past, lk, 0.0)
            hi, lo = _split_bf16(lk)
            later = _dot(ut, hi) + _dot(ut, lo)
            return ls, later, jnp.sum(lk, axis=0, keepdims=True)

        for h, (ls, later, total) in _issue_ahead(KV_GROUP, gates):
            run = run_ref[h:h + 1, :]
            w = jnp.exp(ls + (later + run))
            if diagonal:
                w = jnp.where(past, w, 0.0)
            acc_ref[h] += _dot(vt, w.astype(BF16))
            run_ref[h:h + 1, :] = run + total

    def live():
        return (jnp.max(run_ref[...]) > EXP_UNDERFLOW).astype(jnp.int32)

    chunk(jd, True)

    def cond(c):
        return jnp.logical_and(c[0] >= 0, c[1] > 0)

    def body(c):
        chunk(c[0], False)
        return c[0] - 1, live()

    lax.while_loop(cond, body, (jd - 1, live()))
    _store_heads_t(o_ref, acc_ref, None, KV_GROUP)


def _fox_prompt_kernel(q_ref, k_ref, vt_ref, c_ref, ct_ref, o_ref, qb_ref, m_ref, l_ref, acc_ref, *, tq, tk):
    qi = pl.program_id(2)
    qb_ref[...] = q_ref[...].astype(BF16)
    _init_softmax_state(m_ref, l_ref, acc_ref)
    tpos = _query_positions(qi, tq)
    q0 = pl.multiple_of(qi * tq, tq)
    jd = (qi * tq) // tk

    def chunk(j, diagonal):
        start = pl.multiple_of(j * tk, tk)
        k = k_ref[pl.ds(start, tk), :]
        vt = vt_ref[:, pl.ds(start, tk)]
        ck = c_ref[pl.ds(start, tk), :]
        if diagonal:
            valid = _key_positions(j, tk) <= tpos
        for h, qk in _issue_ahead(KV_GROUP, lambda h: _dot_nt(k, qb_ref[:, _head_cols(h)])):
            cq = ct_ref[h:h + 1, pl.ds(q0, tq)]
            s = qk * ATT_SCALE + cq - ck[:, h:h + 1]
            if diagonal:
                s = jnp.where(valid, s, MASKED)
            _softmax_update_t(s, vt, h, m_ref, l_ref, acc_ref)

    def body(j, carry):
        chunk(j, False)
        return carry

    lax.fori_loop(0, jd, body, 0)
    chunk(jd, True)
    _store_heads_t(o_ref, acc_ref, l_ref, KV_GROUP)


def _moba_select_t(gate, n_blocks_past, n_rows):
    row = lax.broadcasted_iota(jnp.int32, gate.shape, 0)
    is_past = row < n_blocks_past
    gate = jnp.where(is_past, gate, -jnp.inf)
    rank = jnp.zeros(gate.shape, jnp.int32)
    for b2 in range(n_rows):
        g2 = gate[b2:b2 + 1, :]
        beats = (g2 > gate) | ((g2 == gate) & (row > b2))
        rank = rank + beats.astype(jnp.int32)
    return (is_past & (rank < MOBA_TOPK) & (gate > -jnp.inf)).astype(F32)


def _moba_prompt_kernel(q_ref, k_ref, vt_ref, kf_ref, slope_ref, o_ref, qb_ref, kmean_ref, sel_ref,
                        m_ref, l_ref, acc_ref, *, tq, tk, n_blocks):
    qi = pl.program_id(2)

    @pl.when(qi == 0)
    def _():
        kmean_ref[...] = jnp.zeros_like(kmean_ref)
        for b in range(n_blocks):
            blk = kf_ref[b * MOBA_BLOCK:(b + 1) * MOBA_BLOCK, :]
            kmean_ref[b:b + 1, :] = jnp.sum(blk, axis=0, keepdims=True) * (1.0 / MOBA_BLOCK)

    qb_ref[...] = q_ref[...].astype(BF16)
    _init_softmax_state(m_ref, l_ref, acc_ref)
    tpos = _query_positions(qi, tq)
    own = (qi * tq) // MOBA_BLOCK
    kmean = kmean_ref[...]
    for h in range(KV_GROUP):
        sel_ref[h] = _moba_select_t(_dot_nt_precise(kmean, q_ref[:, _head_cols(h)]), own, n_blocks)

    def chunk(j, diagonal):
        start = pl.multiple_of(j * tk, tk)
        k = k_ref[pl.ds(start, tk), :]
        vt = vt_ref[:, pl.ds(start, tk)]
        kpos = _key_positions(j, tk)
        kdist = (kpos - tpos).astype(F32)
        if diagonal:
            valid = kpos <= tpos
        for h, qk in _issue_ahead(KV_GROUP, lambda h: _dot_nt(k, qb_ref[:, _head_cols(h)])):
            s = qk * ATT_SCALE + slope_ref[h:h + 1, :] * kdist
            if not diagonal:
                valid = sel_ref[h, pl.ds(j, 1), :] > 0.0
            s = jnp.where(valid, s, MASKED)
            _softmax_update_t(s, vt, h, m_ref, l_ref, acc_ref)

    def body(j, carry):
        chunk(j, False)
        return carry

    lax.fori_loop(0, own, body, 0)
    chunk(own, True)
    _store_heads_t(o_ref, acc_ref, l_ref, KV_GROUP)


def _gqa_prompt_attention(kind, q, kb, vt, batch, seq, extra=()):
    tq = PROMPT_TQ
    tk = {"sb": SB_TK, "fox": PROMPT_TK, "moba": MOBA_BLOCK}[kind]
    nq = seq // tq
    gw = KV_GROUP * HEAD_DIM
    q_spec = pl.BlockSpec((tq, gw), lambda b, h, i: (b * nq + i, h))
    k_spec = pl.BlockSpec((seq, HEAD_DIM), lambda b, h, i: (b, h))
    vt_spec = pl.BlockSpec((HEAD_DIM, seq), lambda b, h, i: (h, b))
    qb = pltpu.VMEM((tq, gw), BF16)
    row = pltpu.VMEM((KV_GROUP, tq), F32)
    acc = pltpu.VMEM((KV_GROUP, HEAD_DIM, tq), F32)
    if kind == "sb":
        (ut,) = extra
        kern = functools.partial(_sb_prompt_kernel, tq=tq, tk=tk)
        in_specs = [q_spec, k_spec, vt_spec, pl.BlockSpec(ut.shape, lambda b, h, i: (0, 0))]
        scratch = [qb, row, acc]
    elif kind == "fox":
        c, ct = extra
        kern = functools.partial(_fox_prompt_kernel, tq=tq, tk=tk)
        in_specs = [q_spec, k_spec, vt_spec,
                    pl.BlockSpec((None, None, seq, KV_GROUP), lambda b, h, i: (b, h, 0, 0)),
                    pl.BlockSpec((None, None, KV_GROUP, seq), lambda b, h, i: (b, h, 0, 0))]
        scratch = [qb, row, row, acc]
    else:
        kf, slopes = extra
        n_blocks = seq // MOBA_BLOCK
        nb_pad = -(-n_blocks // SUBLANES) * SUBLANES
        kern = functools.partial(_moba_prompt_kernel, tq=tq, tk=tk, n_blocks=n_blocks)
        in_specs = [q_spec, k_spec, vt_spec, k_spec,
                    pl.BlockSpec((None, KV_GROUP, tq), lambda b, h, i: (h, 0, 0))]
        scratch = [qb, pltpu.VMEM((nb_pad, HEAD_DIM), F32), pltpu.VMEM((KV_GROUP, nb_pad, tq), F32),
                   row, row, acc]
    return pl.pallas_call(
        kern, grid=(batch, N_KV_HEADS, nq), in_specs=in_specs, out_specs=q_spec,
        out_shape=jax.ShapeDtypeStruct((batch * seq, N_HEADS * HEAD_DIM), BF16),
        scratch_shapes=scratch,
        compiler_params=_cparams(("arbitrary", "arbitrary", "arbitrary")), name=kind + "_prompt",
    )(q, kb, vt, *extra)


def _mla_kv_up_kernel(c_ref, wk_ref, wvt_ref, kn_ref, vt_ref):
    c = c_ref[...].astype(BF16)
    kn_ref[...] = _dot(c, wk_ref[...]).astype(BF16)
    vt_ref[...] = _dot_nt(wvt_ref[...], c).astype(BF16)


def _mla_kv_up(ckv, w_uk, w_uv_t, tm=256):
    m, r = ckv.shape
    n = w_uk.shape[1]
    const = pl.Buffered(1)
    return pl.pallas_call(
        _mla_kv_up_kernel, grid=(m // tm,),
        in_specs=[pl.BlockSpec((tm, r), lambda i: (i, 0)),
                  pl.BlockSpec((r, n), lambda i: (0, 0), pipeline_mode=const),
                  pl.BlockSpec((n, r), lambda i: (0, 0), pipeline_mode=const)],
        out_specs=[pl.BlockSpec((tm, n), lambda i: (i, 0)), pl.BlockSpec((n, tm), lambda i: (0, i))],
        out_shape=[jax.ShapeDtypeStruct((m, n), BF16), jax.ShapeDtypeStruct((n, m), BF16)],
        compiler_params=_cparams(("arbitrary",)), name="mla_kv_up",
    )(ckv, w_uk, w_uv_t)


def _mla_prompt_kernel(qn_ref, qr_ref, kn_ref, vt_ref, kr_ref, o_ref, m_ref, l_ref, acc_ref, *, tq, tk, nh):
    qi = pl.program_id(2)
    _init_softmax_state(m_ref, l_ref, acc_ref)
    tpos = _query_positions(qi, tq)
    jd = (qi * tq) // tk

    def chunk(j, diagonal):
        start = pl.multiple_of(j * tk, tk)
        kr = kr_ref[pl.ds(start, tk), :].astype(BF16)
        if diagonal:
            valid = _key_positions(j, tk) <= tpos

        def scores(h):
            kn = kn_ref[pl.ds(start, tk), _head_cols(h)]
            return _dot_nt(kn, qn_ref[:, _head_cols(h)]) + _dot_nt(kr, qr_ref[:, _head_cols(h)])

        for h, qk in _issue_ahead(nh, scores):
            vt = vt_ref[_head_cols(h), pl.ds(start, tk)]
            s = qk * MLA_SCALE
            if diagonal:
                s = jnp.where(valid, s, MASKED)
            _softmax_update_t(s, vt, h, m_ref, l_ref, acc_ref)

    def body(j, carry):
        chunk(j, False)
        return carry

    lax.fori_loop(0, jd, body, 0)
    chunk(jd, True)
    _store_heads_t(o_ref, acc_ref, l_ref, nh)


def _mla_prompt_attention(q_nope, q_pe, k_nope, vt, k_pe, batch, seq):
    tq, tk, nh = PROMPT_TQ, PROMPT_TK, MLA_HEADS_PER_STEP
    nq = seq // tq
    gw = nh * HEAD_DIM
    q_spec = pl.BlockSpec((tq, gw), lambda b, h, i: (b * nq + i, h))
    row = pltpu.VMEM((nh, tq), F32)
    return pl.pallas_call(
        functools.partial(_mla_prompt_kernel, tq=tq, tk=tk, nh=nh), grid=(batch, N_HEADS // nh, nq),
        in_specs=[q_spec, q_spec,
                  pl.BlockSpec((seq, gw), lambda b, h, i: (b, h)),
                  pl.BlockSpec((gw, seq), lambda b, h, i: (h, b)),
                  pl.BlockSpec((seq, LANES), lambda b, h, i: (b, 0))],
        out_specs=q_spec,
        out_shape=jax.ShapeDtypeStruct((batch * seq, N_HEADS * MLA_V), BF16),
        scratch_shapes=[row, row, pltpu.VMEM((nh, MLA_V, tq), F32)],
        compiler_params=_cparams(("arbitrary", "arbitrary", "arbitrary")), name="mla_prompt",
    )(q_nope, q_pe, k_nope, vt, k_pe)


PAGES_PER_STEP = 16
SB_HEAD_PAGES = 4
SB_TAIL_PAGES_PER_STEP = 12


def _page_specs(block, layer, n_pages, pages_per_step, first_slot):
    def make(p):
        def index_map(s, g, *prefetch):
            return (layer, prefetch[0][s * n_pages + first_slot(g) + p]) + (0,) * len(block)
        return pl.BlockSpec((None, None) + block, index_map)
    return [make(p) for p in range(pages_per_step)]


def _newest_first(n_groups, pages_per_step):
    return lambda g: (n_groups - 1 - g) * pages_per_step


def _kv_page_head(ref, kvh):
    return ref[pl.ds(kvh, PAGE_SIZE, stride=N_KV_HEADS), :].astype(BF16)


def _gather_kv(page_refs, kvh):
    return jnp.concatenate([_kv_page_head(r, kvh) for r in page_refs], axis=0)


def _sample_row_query(n_q):
    r = lax.broadcasted_iota(jnp.int32, (n_q * KV_GROUP, 1), 0)
    return r // KV_GROUP


def _sb_sample_chunk(kvh, q, k, v, past, u, run_ref, acc_ref):
    sub = u.shape[0]
    z = _dot_nt(q, k) * ATT_SCALE
    ls, lnk = _log_sigmoid_pair(z)
    for sb in range(k.shape[0] // sub - 1, -1, -1):
        cols = slice(sb * sub, (sb + 1) * sub)
        lk = lnk[:, cols] if past is None else jnp.where(past[:, cols], lnk[:, cols], 0.0)
        between = _suffix_exclusive(lk, u) + run_ref[kvh]
        w = jnp.exp(ls[:, cols] + between)
        if past is not None:
            w = jnp.where(past[:, cols], w, 0.0)
        acc_ref[kvh] += _dot(w.astype(BF16), v[cols, :])
        run_ref[kvh] += jnp.sum(lk, axis=1, keepdims=True)


def _sb_sample_head_kernel(pt_ref, q_ref, kn_ref, vn_ref, u_ref, *rest, n_q):
    k_pages = rest[:SB_HEAD_PAGES]
    v_pages = rest[SB_HEAD_PAGES:2 * SB_HEAD_PAGES]
    acc_out, run_out, run_ref, acc_ref = rest[2 * SB_HEAD_PAGES:]
    qidx = _sample_row_query(n_q)
    run_ref[...] = jnp.zeros_like(run_ref)
    acc_ref[...] = jnp.zeros_like(acc_ref)
    lane = lax.broadcasted_iota(jnp.int32, (1, PAGE_SIZE), 1)
    past = lane < qidx
    u_page = u_ref[:PAGE_SIZE, :PAGE_SIZE]
    for kvh in range(N_KV_HEADS):
        q = q_ref[kvh].astype(BF16)
        _sb_sample_chunk(kvh, q, _kv_page_head(kn_ref, kvh), _kv_page_head(vn_ref, kvh), past, u_page,
                         run_ref, acc_ref)
        _sb_sample_chunk(kvh, q, _gather_kv(k_pages, kvh), _gather_kv(v_pages, kvh), None, u_ref[...],
                         run_ref, acc_ref)
    acc_out[...] = acc_ref[...]
    run_out[...] = jnp.broadcast_to(run_ref[...], run_out.shape)


def _sb_sample_tail_kernel(pt_ref, done_ref, q_ref, acc_in, run_in, u_ref, *rest):
    n = SB_TAIL_PAGES_PER_STEP
    k_pages = rest[:n]
    v_pages = rest[n:2 * n]
    o_ref, run_ref, acc_ref = rest[2 * n:]
    s = pl.program_id(0)
    g = pl.program_id(1)

    @pl.when(g == 0)
    def _():
        run_ref[...] = run_in[...][:, :, :1]
        acc_ref[...] = acc_in[...]

    @pl.when(jnp.logical_and(done_ref[s] == 0, jnp.max(run_ref[...]) > EXP_UNDERFLOW))
    def _():
        for kvh in range(N_KV_HEADS):
            q = q_ref[kvh].astype(BF16)
            _sb_sample_chunk(kvh, q, _gather_kv(k_pages, kvh), _gather_kv(v_pages, kvh), None, u_ref[...],
                             run_ref, acc_ref)

    @pl.when(g == pl.num_programs(1) - 1)
    def _():
        o_ref[...] = acc_ref[...].astype(o_ref.dtype)


def _alibi_slopes_rows_sample(kvh, n_q):
    r = lax.broadcasted_iota(jnp.int32, (n_q * KV_GROUP, 1), 0)
    h = kvh * KV_GROUP + r % KV_GROUP
    return jnp.exp2(-8.0 * (h + 1).astype(F32) / N_HEADS)


def _softmax_sample_kernel(pt_ref, q_ref, kn_ref, vn_ref, *rest, n_q, kind, n_groups):
    if kind == "fox":
        fnew_ref = rest[0]
        f_pages = rest[1:1 + PAGES_PER_STEP]
        rest = rest[1 + PAGES_PER_STEP:]
    else:
        sel_ref = rest[0]
        rest = rest[1:]
    k_pages = rest[:PAGES_PER_STEP]
    v_pages = rest[PAGES_PER_STEP:2 * PAGES_PER_STEP]
    o_ref, m_ref, l_ref, acc_ref = rest[2 * PAGES_PER_STEP:2 * PAGES_PER_STEP + 4]
    fcarry_ref = rest[2 * PAGES_PER_STEP + 4] if kind == "fox" else None
    g = pl.program_id(1)
    rows = n_q * KV_GROUP
    qidx = _sample_row_query(n_q)
    chunk_keys = PAGES_PER_STEP * PAGE_SIZE
    past_len = n_groups * chunk_keys
    first_slot = (n_groups - 1 - g) * PAGES_PER_STEP

    @pl.when(g == 0)
    def _():
        m_ref[...] = jnp.full_like(m_ref, MASKED)
        l_ref[...] = jnp.zeros_like(l_ref)
        acc_ref[...] = jnp.zeros_like(acc_ref)
        lane = lax.broadcasted_iota(jnp.int32, (1, PAGE_SIZE), 1)
        valid = lane <= qidx
        if kind == "fox":
            fcarry_ref[...] = jnp.zeros_like(fcarry_ref)
        for kvh in range(N_KV_HEADS):
            q = q_ref[kvh].astype(BF16)
            s = _dot_nt(q, _kv_page_head(kn_ref, kvh)) * ATT_SCALE
            if kind == "fox":
                f = fnew_ref[kvh]
                bias = jnp.zeros((rows, PAGE_SIZE), F32)
                for j in range(n_q):
                    between = (lane > j) & (lane <= qidx)
                    bj = jnp.sum(jnp.where(between, f, 0.0), axis=1, keepdims=True)
                    bias = jnp.where(lane == j, bj, bias)
                s = s + bias
            else:
                slope = _alibi_slopes_rows_sample(kvh, n_q)
                s = s - slope * (qidx - lane).astype(F32)
            s = jnp.where(valid, s, MASKED)
            _softmax_update(s, valid, _kv_page_head(vn_ref, kvh), m_ref.at[kvh], l_ref.at[kvh], acc_ref.at[kvh])

    if kind == "fox":
        f = jnp.concatenate([r[...] for r in f_pages], axis=1)
        lane_c = lax.broadcasted_iota(jnp.int32, f.shape, 1)
        incl = f
        d = 1
        while d < chunk_keys:
            shifted = pltpu.roll(incl, chunk_keys - d, axis=1)
            incl = incl + jnp.where(lane_c + d < chunk_keys, shifted, 0.0)
            d *= 2
        suffix = incl - f + fcarry_ref[...]
        lane = lax.broadcasted_iota(jnp.int32, (1, PAGE_SIZE), 1)

    qk = [_dot_nt(q_ref[kvh].astype(BF16), _gather_kv(k_pages, kvh)) for kvh in range(N_KV_HEADS)]
    for kvh in range(N_KV_HEADS):
        s = qk[kvh] * ATT_SCALE
        if kind == "fox":
            f_new = fnew_ref[kvh]
            upto_q = jnp.sum(jnp.where(lane <= qidx, f_new, 0.0), axis=1, keepdims=True)
            sfx = suffix[kvh * KV_GROUP:(kvh + 1) * KV_GROUP, :]
            s = s + jnp.concatenate([sfx] * n_q, axis=0) + upto_q
            valid = None
        else:
            slope = _alibi_slopes_rows_sample(kvh, n_q)
            kpos = first_slot * PAGE_SIZE + lax.broadcasted_iota(jnp.int32, (1, chunk_keys), 1)
            s = s - slope * (past_len + qidx - kpos).astype(F32)
            sel = sel_ref[kvh]
            sel_lane = lax.broadcasted_iota(jnp.int32, sel.shape, 1)
            pages_per_block = MOBA_BLOCK // PAGE_SIZE
            cols = []
            for bb in range(PAGES_PER_STEP // pages_per_block):
                blk = first_slot // pages_per_block + bb
                picked = jnp.sum(jnp.where(sel_lane == blk, sel, 0.0), axis=1, keepdims=True) > 0.0
                cols.append(jnp.broadcast_to(picked, (rows, MOBA_BLOCK)))
            valid = jnp.concatenate(cols, axis=1)
            s = jnp.where(valid, s, MASKED)
        _softmax_update(s, valid, _gather_kv(v_pages, kvh), m_ref.at[kvh], l_ref.at[kvh], acc_ref.at[kvh])

    if kind == "fox":
        fcarry_ref[...] += jnp.sum(f, axis=1, keepdims=True)

    @pl.when(g == pl.num_programs(1) - 1)
    def _():
        o_ref[...] = (acc_ref[...] / l_ref[...]).astype(o_ref.dtype)


def _moba_select(gate, n_blocks_past, n_cols):
    lane = lax.broadcasted_iota(jnp.int32, gate.shape, 1)
    is_past = lane < n_blocks_past
    gate = jnp.where(is_past, gate, -jnp.inf)
    rank = jnp.zeros(gate.shape, jnp.int32)
    for b2 in range(n_cols):
        g2 = gate[:, b2:b2 + 1]
        beats = (g2 > gate) | ((g2 == gate) & (lane > b2))
        rank = rank + beats.astype(jnp.int32)
    return (is_past & (rank < MOBA_TOPK) & (gate > -jnp.inf)).astype(F32)


def _moba_sample_select_kernel(pt_ref, q_ref, *rest, n_groups):
    k_pages = rest[:PAGES_PER_STEP]
    sel_ref, kmean_ref = rest[PAGES_PER_STEP:]
    g = pl.program_id(1)
    pages_per_block = MOBA_BLOCK // PAGE_SIZE
    blocks_per_step = PAGES_PER_STEP // pages_per_block

    @pl.when(g == 0)
    def _():
        kmean_ref[...] = jnp.zeros_like(kmean_ref)

    first_block = pl.multiple_of((n_groups - 1 - g) * blocks_per_step, blocks_per_step)
    for kvh in range(N_KV_HEADS):
        means = []
        for bb in range(blocks_per_step):
            tot = None
            for p in range(pages_per_block):
                page = k_pages[bb * pages_per_block + p][pl.ds(kvh, PAGE_SIZE, stride=N_KV_HEADS), :]
                part = jnp.sum(page, axis=0, keepdims=True)
                tot = part if tot is None else tot + part
            means.append(tot * (1.0 / MOBA_BLOCK))
        kmean_ref[kvh, pl.ds(first_block, blocks_per_step), :] = jnp.concatenate(means, axis=0)

    @pl.when(g == n_groups - 1)
    def _():
        n_blocks = n_groups * blocks_per_step
        for kvh in range(N_KV_HEADS):
            gate = _dot_nt_precise(q_ref[kvh], kmean_ref[kvh])
            sel_ref[kvh] = _moba_select(gate, n_blocks, n_blocks)


def _sample_call(kern, name, prefetch, grid, fixed_ins, fixed_specs, paged_ins, paged_specs,
                 out_shape, out_spec, scratch):
    ins = list(fixed_ins)
    specs = list(fixed_specs)
    for arr, sp in zip(paged_ins, paged_specs):
        ins += [arr] * len(sp)
        specs += sp
    return pl.pallas_call(
        kern,
        grid_spec=pltpu.PrefetchScalarGridSpec(
            num_scalar_prefetch=len(prefetch), grid=grid, in_specs=specs, out_specs=out_spec,
            scratch_shapes=scratch),
        out_shape=out_shape,
        compiler_params=_cparams(("arbitrary", "arbitrary")), name=name,
    )(*prefetch, *ins)


def _seq_spec(shape):
    nd = len(shape)
    return pl.BlockSpec((None,) + shape, lambda s, g, *prefetch: (s,) + (0,) * nd)


def _const_spec(shape):
    nd = len(shape)
    return pl.BlockSpec(shape, lambda s, g, *prefetch: (0,) * nd)


def _sb_sample_attention(page_table, q_rows, k_new_page, v_new_page, cache_k, cache_v, layer, u):
    n_seq, _, rows, _ = q_rows.shape
    n_q = rows // KV_GROUP
    n_pages = page_table.shape[1]
    n_tail = n_pages - SB_HEAD_PAGES
    assert n_tail % SB_TAIL_PAGES_PER_STEP == 0
    tail_groups = n_tail // SB_TAIL_PAGES_PER_STEP
    kv_block = (PAGE_SIZE * N_KV_HEADS, HEAD_DIM)
    q_spec = _seq_spec((N_KV_HEADS, rows, HEAD_DIM))
    new_spec = _seq_spec(kv_block)
    state_shape = jax.ShapeDtypeStruct((n_seq, N_KV_HEADS, rows, HEAD_DIM), F32)
    col = pltpu.VMEM((N_KV_HEADS, rows, 1), F32)
    acc = pltpu.VMEM((N_KV_HEADS, rows, HEAD_DIM), F32)
    head_specs = _page_specs(kv_block, layer, n_pages, SB_HEAD_PAGES, lambda g: n_tail)
    acc0, run0 = _sample_call(
        functools.partial(_sb_sample_head_kernel, n_q=n_q), "sb_sample_head", [page_table.reshape(-1)],
        (n_seq, 1), [q_rows, k_new_page, v_new_page, u], [q_spec, new_spec, new_spec, _const_spec(u.shape)],
        [cache_k, cache_v], [head_specs, head_specs], [state_shape, state_shape], [q_spec, q_spec], [col, acc])
    done = jnp.max(run0, axis=(1, 2, 3)) < EXP_UNDERFLOW
    pt_tail = jnp.where(done[:, None], page_table[0, 0], page_table).reshape(-1)
    tail_specs = _page_specs(kv_block, layer, n_pages, SB_TAIL_PAGES_PER_STEP,
                             _newest_first(tail_groups, SB_TAIL_PAGES_PER_STEP))
    return _sample_call(
        _sb_sample_tail_kernel, "sb_sample_tail", [pt_tail, done.astype(jnp.int32)], (n_seq, tail_groups),
        [q_rows, acc0, run0, u], [q_spec, q_spec, q_spec, _const_spec(u.shape)],
        [cache_k, cache_v], [tail_specs, tail_specs],
        jax.ShapeDtypeStruct((n_seq, N_KV_HEADS, rows, HEAD_DIM), BF16), q_spec, [col, acc])


def _gqa_sample_attention(kind, page_table, q_rows, k_new_page, v_new_page, cache_k, cache_v, layer,
                          extra=None):
    n_seq, _, rows, _ = q_rows.shape
    n_q = rows // KV_GROUP
    n_pages = page_table.shape[1]
    n_groups = n_pages // PAGES_PER_STEP
    grid = (n_seq, n_groups)
    prefetch = [page_table.reshape(-1)]
    first_slot = _newest_first(n_groups, PAGES_PER_STEP)
    kv_block = (PAGE_SIZE * N_KV_HEADS, HEAD_DIM)
    kv_specs = _page_specs(kv_block, layer, n_pages, PAGES_PER_STEP, first_slot)
    q_spec = _seq_spec((N_KV_HEADS, rows, HEAD_DIM))
    new_spec = _seq_spec(kv_block)
    out_shape = jax.ShapeDtypeStruct((n_seq, N_KV_HEADS, rows, HEAD_DIM), BF16)
    col = pltpu.VMEM((N_KV_HEADS, rows, 1), F32)
    acc = pltpu.VMEM((N_KV_HEADS, rows, HEAD_DIM), F32)
    kern = functools.partial(_softmax_sample_kernel, n_q=n_q, kind=kind, n_groups=n_groups)
    if kind == "fox":
        f_new_rows, cache_logf_t = extra
        f_specs = _page_specs((N_HEADS, PAGE_SIZE), layer, n_pages, PAGES_PER_STEP, first_slot)
        return _sample_call(
            kern, "fox_sample", prefetch, grid,
            [q_rows, k_new_page, v_new_page, f_new_rows],
            [q_spec, new_spec, new_spec, _seq_spec((N_KV_HEADS, rows, LANES))],
            [cache_logf_t, cache_k, cache_v], [f_specs, kv_specs, kv_specs], out_shape, q_spec,
            [col, col, acc, pltpu.VMEM((N_HEADS, 1), F32)])
    sel = _sample_call(
        functools.partial(_moba_sample_select_kernel, n_groups=n_groups), "moba_sample_select",
        prefetch, grid, [q_rows], [q_spec], [cache_k], [kv_specs],
        jax.ShapeDtypeStruct((n_seq, N_KV_HEADS, rows, LANES), F32),
        _seq_spec((N_KV_HEADS, rows, LANES)), [pltpu.VMEM((N_KV_HEADS, LANES, HEAD_DIM), F32)])
    return _sample_call(
        kern, "moba_sample", prefetch, grid,
        [q_rows, k_new_page, v_new_page, sel],
        [q_spec, new_spec, new_spec, _seq_spec((N_KV_HEADS, rows, LANES))],
        [cache_k, cache_v], [kv_specs, kv_specs], out_shape, q_spec, [col, col, acc])


def _mla_sample_kernel(pt_ref, ql_ref, qr_ref, cn_ref, rn_ref, *rest, n_q):
    c_pages = rest[:PAGES_PER_STEP]
    r_pages = rest[PAGES_PER_STEP:2 * PAGES_PER_STEP]
    o_ref, m_ref, l_ref, acc_ref = rest[2 * PAGES_PER_STEP:]
    g = pl.program_id(1)
    ql = ql_ref[...]
    qr = qr_ref[...][:, :MLA_ROPE]
    r = lax.broadcasted_iota(jnp.int32, (ql.shape[0], 1), 0)
    qidx = r % n_q

    @pl.when(g == 0)
    def _():
        m_ref[...] = jnp.full_like(m_ref, MASKED)
        l_ref[...] = jnp.zeros_like(l_ref)
        acc_ref[...] = jnp.zeros_like(acc_ref)
        c = cn_ref[...].astype(BF16)
        s = (_dot_nt(ql, c) + _dot_nt(qr, rn_ref[...].astype(BF16))) * MLA_SCALE
        lane = lax.broadcasted_iota(jnp.int32, (1, PAGE_SIZE), 1)
        valid = lane <= qidx
        s = jnp.where(valid, s, MASKED)
        _softmax_update(s, valid, c, m_ref, l_ref, acc_ref)

    c = jnp.concatenate([p[...].astype(BF16) for p in c_pages], axis=0)
    kr = jnp.concatenate([p[...].astype(BF16) for p in r_pages], axis=0)
    s = (_dot_nt(ql, c) + _dot_nt(qr, kr)) * MLA_SCALE
    _softmax_update(s, None, c, m_ref, l_ref, acc_ref)

    @pl.when(g == pl.num_programs(1) - 1)
    def _():
        o_ref[...] = (acc_ref[...] / l_ref[...]).astype(o_ref.dtype)


def _mla_sample_attention(page_table, q_lat, q_pe, c_new_page, r_new_page, cache_ckv, cache_kpe, layer):
    n_seq, rows, rank = q_lat.shape
    n_q = rows // N_HEADS
    n_pages = page_table.shape[1]
    n_groups = n_pages // PAGES_PER_STEP
    first_slot = _newest_first(n_groups, PAGES_PER_STEP)
    c_block = (PAGE_SIZE, rank)
    r_block = (PAGE_SIZE, MLA_ROPE)
    return _sample_call(
        functools.partial(_mla_sample_kernel, n_q=n_q), "mla_sample", [page_table.reshape(-1)],
        (n_seq, n_groups), [q_lat, q_pe, c_new_page, r_new_page],
        [_seq_spec((rows, rank)), _seq_spec((rows, LANES)), _seq_spec(c_block), _seq_spec(r_block)],
        [cache_ckv, cache_kpe],
        [_page_specs(c_block, layer, n_pages, PAGES_PER_STEP, first_slot),
         _page_specs(r_block, layer, n_pages, PAGES_PER_STEP, first_slot)],
        jax.ShapeDtypeStruct((n_seq, rows, rank), BF16), _seq_spec((rows, rank)),
        [pltpu.VMEM((rows, 1), F32), pltpu.VMEM((rows, 1), F32), pltpu.VMEM((rows, rank), F32)])


def _cumsum_lanes_kernel(x_ref, o_ref):
    x = x_ref[...]
    n = x.shape[1]
    lane = lax.broadcasted_iota(jnp.int32, x.shape, 1)
    d = 1
    while d < n:
        x = x + jnp.where(lane >= d, pltpu.roll(x, d, axis=1), 0.0)
        d *= 2
    o_ref[...] = x


def _cumsum_lanes(x):
    b, r, t = x.shape
    return pl.pallas_call(
        _cumsum_lanes_kernel, grid=(b,),
        in_specs=[pl.BlockSpec((None, r, t), lambda i: (i, 0, 0))],
        out_specs=pl.BlockSpec((None, r, t), lambda i: (i, 0, 0)),
        out_shape=jax.ShapeDtypeStruct(x.shape, F32),
        compiler_params=_cparams(("arbitrary",)), name="cumsum_lanes",
    )(x)


def _sample_rows(x, n_seq, n_q):
    x = x.reshape(n_seq, n_q, N_KV_HEADS, KV_GROUP, HEAD_DIM)
    return jnp.transpose(x, (0, 2, 1, 3, 4)).reshape(n_seq, N_KV_HEADS, n_q * KV_GROUP, HEAD_DIM)


def _sample_rows_back(o, n_seq, n_q):
    o = o.reshape(n_seq, N_KV_HEADS, n_q, KV_GROUP, HEAD_DIM)
    return jnp.transpose(o, (0, 2, 1, 3, 4)).reshape(n_seq * n_q, N_HEADS * HEAD_DIM)


def _new_page(rows, n_seq, n_q):
    w = rows.shape[1]
    per_tok = w // HEAD_DIM
    page = rows.reshape(n_seq, n_q * per_tok, HEAD_DIM)
    return jnp.pad(page, ((0, 0), (0, (PAGE_SIZE - n_q) * per_tok), (0, 0)))


def _suffix_matrix(n):
    j = lax.broadcasted_iota(jnp.int32, (n, n), 0)
    s = lax.broadcasted_iota(jnp.int32, (n, n), 1)
    return (j > s).astype(BF16)


def _rope_tables(pos):
    half = MLA_ROPE // 2
    inv = ROPE_THETA ** (-jnp.arange(half, dtype=F32) / half)
    ang = pos.astype(F32)[:, None] * inv[None, :]
    zeros = jnp.zeros((pos.shape[0], LANES - MLA_ROPE), F32)
    cos, sin = jnp.cos(ang), jnp.sin(ang)
    return jnp.concatenate([cos, cos, zeros], 1), jnp.concatenate([sin, sin, zeros], 1)


def _rotate_half_weights(w):
    half = MLA_ROPE // 2
    x1, x2 = w[..., :half], w[..., half:]
    pad = jnp.zeros(w.shape[:-1] + (LANES - MLA_ROPE,), w.dtype)
    return jnp.concatenate([x1, x2, pad], -1), jnp.concatenate([-x2, x1, pad], -1)


def kernel(x_prompt, x_sample, cache_sb_k, cache_sb_v, cache_mla_ckv, cache_mla_kpe, cache_fox_k, cache_fox_v, cache_fox_logf, cache_moba_k, cache_moba_v, page_table, ln_g, ln_b, sb_w_qkv, sb_w_o, mla_w_dq, mla_q_norm, mla_w_uq, mla_w_dkv, mla_kv_norm, mla_w_ukv, mla_w_o, fox_w_qkv, fox_w_f, fox_b_f, fox_w_o, moba_w_qkv, moba_w_o, router_w, router_b, moe_w_gate, moe_w_up, moe_w_down):
    batch, seq, d_model = x_prompt.shape
    n_seq, n_q, _ = x_sample.shape
    n_pages = page_table.shape[1]
    past_len = n_pages * PAGE_SIZE
    mp = batch * seq
    ms = n_seq * n_q
    m = mp + ms
    nq_w = N_HEADS * HEAD_DIM
    nk_w = N_KV_HEADS * HEAD_DIM

    x = jnp.concatenate([x_prompt.reshape(mp, d_model), x_sample.reshape(ms, d_model)], axis=0)
    u = _suffix_matrix(MOBA_BLOCK)
    rw_hi, rw_lo = _split_bf16(router_w.T)
    rb = router_b.reshape(N_EXPERTS, 1).astype(F32)
    pos = jnp.concatenate([jnp.tile(jnp.arange(seq, dtype=jnp.int32), batch),
                           jnp.tile(past_len + jnp.arange(n_q, dtype=jnp.int32), n_seq)])
    cos_t, sin_t = _rope_tables(pos)
    alibi = jnp.exp2(-8.0 * jnp.arange(1, N_HEADS + 1, dtype=F32) / N_HEADS)
    alibi_rows = jnp.broadcast_to(alibi.reshape(N_KV_HEADS, KV_GROUP, 1), (N_KV_HEADS, KV_GROUP, PROMPT_TQ))

    def kv_pages(cache):
        return cache.reshape(cache.shape[0], cache.shape[1], PAGE_SIZE * N_KV_HEADS, HEAD_DIM)

    def kv_outputs(k, v):
        kp = k[:mp].reshape(1, batch, seq, N_KV_HEADS, HEAD_DIM)
        vp = v[:mp].reshape(1, batch, seq, N_KV_HEADS, HEAD_DIM)
        ks = k[mp:].reshape(1, n_seq, n_q, N_KV_HEADS, HEAD_DIM)
        vs = v[mp:].reshape(1, n_seq, n_q, N_KV_HEADS, HEAD_DIM)
        return kp, vp, ks, vs

    def gqa_layer(kind, w_qkv, cache_k, cache_v, w_f=None, b_f=None, cache_logf=None):
        outs = [(nq_w, F32), (nk_w, F32), (nk_w, F32)]
        t_outs = [(nk_w, BF16)]
        if kind == "fox":
            w = jnp.concatenate([w_qkv, jnp.pad(w_f, ((0, 0), (0, LANES - N_HEADS)))], axis=1).astype(BF16)
            bf = jnp.pad(b_f.astype(F32), (0, LANES - N_HEADS)).reshape(1, LANES)
            q, k, v, logf, kb, vt = _fused_matmul(
                x, w, _qkv_fox_epilogue, outs + [(LANES, F32), (nk_w, BF16)], const_ins=(bf,),
                t_outs=t_outs, name="qkv_fox")
            logf = logf[:, :N_HEADS]
        else:
            q, k, v, kb, vt = _fused_matmul(x, w_qkv.astype(BF16), _qkv_epilogue, outs + [(nk_w, BF16)],
                                            t_outs=t_outs, name="qkv_" + kind)
        q_rows = _sample_rows(q[mp:], n_seq, n_q)
        k_new = _new_page(k[mp:], n_seq, n_q)
        v_new = _new_page(v[mp:], n_seq, n_q)
        if kind == "sb":
            op = _gqa_prompt_attention("sb", q, kb, vt, batch, seq, (_suffix_matrix(SB_TK).T,))
            o_s = _sb_sample_attention(page_table, q_rows, k_new, v_new, kv_pages(cache_k),
                                       kv_pages(cache_v), 0, u)
            state = kv_outputs(k, v)
        elif kind == "fox":
            lf_p = logf[:mp].reshape(batch, seq, N_HEADS)
            c_t = _cumsum_lanes(jnp.transpose(lf_p, (0, 2, 1)))
            c_t = c_t.reshape(batch, N_KV_HEADS, KV_GROUP, seq)
            c = jnp.transpose(c_t, (0, 1, 3, 2))
            op = _gqa_prompt_attention("fox", q, kb, vt, batch, seq, (c, c_t))
            lf_s = logf[mp:].reshape(n_seq, n_q, N_KV_HEADS, KV_GROUP)
            f_rows = jnp.transpose(lf_s, (0, 2, 3, 1))
            f_rows = jnp.broadcast_to(f_rows[:, :, None], (n_seq, N_KV_HEADS, n_q, KV_GROUP, n_q))
            f_rows = f_rows.reshape(n_seq, N_KV_HEADS, n_q * KV_GROUP, n_q)
            f_rows = jnp.pad(f_rows, ((0, 0), (0, 0), (0, 0), (0, LANES - n_q)))
            logf_t = jnp.transpose(cache_logf, (0, 1, 3, 2))
            o_s = _gqa_sample_attention("fox", page_table, q_rows, k_new, v_new, kv_pages(cache_k),
                                        kv_pages(cache_v), 0, (f_rows, logf_t))
            kp, vp, ks, vs = kv_outputs(k, v)
            state = (kp, vp, lf_p[None], ks, vs, logf[mp:].reshape(1, n_seq, n_q, N_HEADS))
        else:
            op = _gqa_prompt_attention("moba", q, kb, vt, batch, seq, (k, alibi_rows))
            o_s = _gqa_sample_attention("moba", page_table, q_rows, k_new, v_new, kv_pages(cache_k),
                                        kv_pages(cache_v), 0)
            state = kv_outputs(k, v)
        o = jnp.concatenate([op, _sample_rows_back(o_s, n_seq, n_q)], axis=0)
        return o, state

    def mla_layer(w_dq, q_norm, w_uq, w_dkv, kv_norm, w_ukv, cache_ckv, cache_kpe):
        q_rank = w_dq.shape[1]
        kv_rank = kv_norm.shape[0]
        ka, kb = _rotate_half_weights(w_dkv[:, kv_rank:])
        w_down = jnp.concatenate([w_dq, w_dkv[:, :kv_rank], ka, kb], axis=1).astype(BF16)
        cq, ckv, kpe = _fused_matmul(
            x, w_down, _mla_down_epilogue, [(q_rank, BF16), (kv_rank, F32), (LANES, F32)],
            row_ins=(cos_t, sin_t), const_ins=(q_norm.reshape(1, -1), kv_norm.reshape(1, -1)), name="mla_down")
        w_uq3 = w_uq.reshape(q_rank, N_HEADS, MLA_NOPE + MLA_ROPE)
        qa, qb = _rotate_half_weights(w_uq3[..., MLA_NOPE:])
        w_q = jnp.concatenate([w_uq3[..., :MLA_NOPE].reshape(q_rank, -1), qa.reshape(q_rank, -1),
                               qb.reshape(q_rank, -1)], axis=1).astype(BF16)
        q_nope, q_pe = _fused_matmul(cq, w_q, _mla_q_epilogue, [(nq_w, BF16), (nq_w, BF16)],
                                     row_ins=(cos_t, sin_t), name="mla_q")
        w_ukv3 = w_ukv.reshape(kv_rank, N_HEADS, MLA_NOPE + MLA_V)
        w_uk = w_ukv3[..., :MLA_NOPE]
        w_uv = w_ukv3[..., MLA_NOPE:]
        k_nope, v_t = _mla_kv_up(ckv[:mp], w_uk.reshape(kv_rank, -1).astype(BF16),
                                 jnp.transpose(w_uv, (1, 2, 0)).reshape(-1, kv_rank).astype(BF16))
        op = _mla_prompt_attention(q_nope, q_pe, k_nope, v_t, kpe, batch, seq)
        w_uk_t = jnp.transpose(w_uk, (1, 2, 0))
        w_uv_h = jnp.transpose(w_uv, (1, 0, 2))
        qn_s = jnp.transpose(q_nope[mp:].reshape(ms, N_HEADS, MLA_NOPE), (1, 0, 2))
        q_lat = _batched_matmul(qn_s, w_uk_t, BF16, "mla_q_lat")
        q_lat = jnp.transpose(q_lat.reshape(N_HEADS, n_seq, n_q, kv_rank), (1, 0, 2, 3))
        q_lat = q_lat.reshape(n_seq, N_HEADS * n_q, kv_rank)
        qr_s = jnp.transpose(q_pe[mp:].reshape(n_seq, n_q, N_HEADS, HEAD_DIM), (0, 2, 1, 3))
        qr_s = qr_s.reshape(n_seq, N_HEADS * n_q, HEAD_DIM)
        c_new = jnp.pad(ckv[mp:].reshape(n_seq, n_q, kv_rank), ((0, 0), (0, PAGE_SIZE - n_q), (0, 0)))
        r_new = jnp.pad(kpe[mp:, :MLA_ROPE].reshape(n_seq, n_q, MLA_ROPE), ((0, 0), (0, PAGE_SIZE - n_q), (0, 0)))
        o_lat = _mla_sample_attention(page_table, q_lat, qr_s, c_new, r_new, cache_ckv, cache_kpe, 0)
        o_lat = jnp.transpose(o_lat.reshape(n_seq, N_HEADS, n_q, kv_rank), (1, 0, 2, 3))
        o_s = _batched_matmul(o_lat.reshape(N_HEADS, ms, kv_rank), w_uv_h, BF16, "mla_o_up")
        o_s = jnp.transpose(o_s, (1, 0, 2)).reshape(ms, N_HEADS * MLA_V)
        o = jnp.concatenate([op, o_s], axis=0)
        state = (ckv[:mp].reshape(1, batch, seq, kv_rank), kpe[:mp, :MLA_ROPE].reshape(1, batch, seq, MLA_ROPE),
                 ckv[mp:].reshape(1, n_seq, n_q, kv_rank), kpe[mp:, :MLA_ROPE].reshape(1, n_seq, n_q, MLA_ROPE))
        return o, state

    moe_tm = m // 16
    states = {}
    for layer in range(DEPTH):
        kind = layer % 4
        if kind == 0:
            o, states["sb"] = gqa_layer("sb", sb_w_qkv[0], cache_sb_k, cache_sb_v)
            w_o = sb_w_o[0]
        elif kind == 1:
            o, states["mla"] = mla_layer(mla_w_dq[0], mla_q_norm[0], mla_w_uq[0], mla_w_dkv[0], mla_kv_norm[0],
                                         mla_w_ukv[0], cache_mla_ckv, cache_mla_kpe)
            w_o = mla_w_o[0]
        elif kind == 2:
            o, states["fox"] = gqa_layer("fox", fox_w_qkv[0], cache_fox_k, cache_fox_v, fox_w_f[0], fox_b_f[0],
                                         cache_fox_logf)
            w_o = fox_w_o[0]
        else:
            o, states["moba"] = gqa_layer("moba", moba_w_qkv[0], cache_moba_k, cache_moba_v)
            w_o = moba_w_o[0]
        x1, gates_t = _oproj_norm_route(o, w_o.astype(BF16), x, ln_g[layer, 0].reshape(1, -1),
                                        ln_b[layer, 0].reshape(1, -1), rw_hi, rw_lo, rb)
        x = _moe_norm(x1, gates_t.T, moe_w_gate[layer].astype(BF16), moe_w_up[layer].astype(BF16),
                      moe_w_down[layer].astype(BF16), ln_g[layer, 1].reshape(1, -1),
                      ln_b[layer, 1].reshape(1, -1), moe_tm)

    y_prompt = x[:mp].reshape(batch, seq, d_model)
    y_sample = x[mp:].reshape(n_seq, n_q, d_model)
    return (y_prompt, y_sample) + states["sb"] + states["mla"] + states["fox"] + states["moba"]
```

```python
import functools

import jax
import jax.numpy as jnp
from jax import lax
from jax.experimental import pallas as pl
from jax.experimental.pallas import tpu as pltpu

F32 = jnp.float32
BF16 = jnp.bfloat16

HEAD_DIM = 128
N_HEADS = 16
N_KV_HEADS = 2
KV_GROUP = N_HEADS // N_KV_HEADS
PAGE_SIZE = 128
MLA_NOPE = 128
MLA_ROPE = 64
MLA_V = 128
MLA_SCALE = (MLA_NOPE + MLA_ROPE) ** -0.5
ROPE_THETA = 10000.0
MOBA_BLOCK = 256
MOBA_TOPK = 3
N_EXPERTS = 16
EXPERTS_PER_GROUP = 4
N_GROUPS = N_EXPERTS // EXPERTS_PER_GROUP
DEPTH = 4
DN_ALPHA = (2 * DEPTH) ** 0.25
LN_EPS = 1e-5
RMS_EPS = 1e-6
ATT_SCALE = HEAD_DIM ** -0.5

LANES = 128
SUBLANES = 8
MASKED = -1e30
EXP_UNDERFLOW = -104.0
VMEM_LIMIT_BYTES = 56 * 1024 * 1024

NT_DIMS = (((1,), (1,)), ((), ()))


def _cparams(semantics):
    return pltpu.CompilerParams(dimension_semantics=semantics, vmem_limit_bytes=VMEM_LIMIT_BYTES)


def _dot(a, b):
    return jnp.dot(a, b, preferred_element_type=F32)


def _dot_nt(a, b):
    return lax.dot_general(a, b, NT_DIMS, preferred_element_type=F32)


def _split_bf16(x):
    hi = x.astype(BF16)
    lo = (x - hi.astype(F32)).astype(BF16)
    return hi, lo


def _dot_nt_precise(a, b):
    a_hi, a_lo = _split_bf16(a)
    b_hi, b_lo = _split_bf16(b)
    return _dot_nt(a_hi, b_hi) + _dot_nt(a_lo, b_hi) + _dot_nt(a_hi, b_lo)


def _log_sigmoid_pair(z):
    sp = jnp.log(1.0 + jnp.exp(-jnp.abs(z)))
    return jnp.minimum(z, 0.0) - sp, -jnp.maximum(z, 0.0) - sp


def _suffix_exclusive(x, u):
    hi, lo = _split_bf16(x)
    return _dot(hi, u) + _dot(lo, u)


def _layer_norm(y, g, b):
    mean = jnp.mean(y, axis=-1, keepdims=True)
    yc = y - mean
    var = jnp.mean(yc * yc, axis=-1, keepdims=True)
    return yc * lax.rsqrt(var + LN_EPS) * g + b


def _rms_norm(y, g):
    return y * lax.rsqrt(jnp.mean(y * y, axis=-1, keepdims=True) + RMS_EPS) * g


def _softmax_update(s, valid, v, m_ref, l_ref, acc_ref):
    m_old = m_ref[...]
    m_new = jnp.maximum(m_old, jnp.max(s, axis=1, keepdims=True))
    alpha = jnp.exp(m_old - m_new)
    p = jnp.exp(s - m_new)
    if valid is not None:
        p = jnp.where(valid, p, 0.0)
    l_ref[...] = alpha * l_ref[...] + jnp.sum(p, axis=1, keepdims=True)
    acc_ref[...] = alpha * acc_ref[...] + _dot(p.astype(BF16), v)
    m_ref[...] = m_new


def _fused_matmul_kernel(*refs, n_row, n_const, epilogue):
    x_ref, w_ref = refs[0], refs[1]
    row_refs = refs[2:2 + n_row]
    const_refs = refs[2 + n_row:2 + n_row + n_const]
    out_refs = refs[2 + n_row + n_const:]
    acc = _dot(x_ref[...].astype(BF16), w_ref[...])
    outs = epilogue(acc, [r[...] for r in row_refs], [c[...] for c in const_refs])
    for o_ref, val in zip(out_refs, outs):
        o_ref[...] = val.astype(o_ref.dtype)


def _fused_matmul(x, w, epilogue, outs, row_ins=(), const_ins=(), t_outs=(), tm=256, name="fused_matmul"):
    m, k = x.shape
    n = w.shape[1]
    assert m % tm == 0 and w.shape[0] == k
    in_specs = [pl.BlockSpec((tm, k), lambda i: (i, 0)),
                pl.BlockSpec((k, n), lambda i: (0, 0), pipeline_mode=pl.Buffered(1))]
    in_specs += [pl.BlockSpec((tm, a.shape[1]), lambda i: (i, 0)) for a in row_ins]
    in_specs += [pl.BlockSpec(c.shape, lambda i: (0, 0)) for c in const_ins]
    out_shape = [jax.ShapeDtypeStruct((m, width), dt) for width, dt in outs]
    out_specs = [pl.BlockSpec((tm, width), lambda i: (i, 0)) for width, _ in outs]
    out_shape += [jax.ShapeDtypeStruct((rows, m), dt) for rows, dt in t_outs]
    out_specs += [pl.BlockSpec((rows, tm), lambda i: (0, i)) for rows, _ in t_outs]
    return pl.pallas_call(
        functools.partial(_fused_matmul_kernel, n_row=len(row_ins), n_const=len(const_ins),
                          epilogue=epilogue),
        grid=(m // tm,), in_specs=in_specs, out_specs=out_specs, out_shape=out_shape,
        compiler_params=_cparams(("arbitrary",)), name=name,
    )(x, w, *row_ins, *const_ins)


def _bmm_kernel(x_ref, w_ref, o_ref):
    o_ref[...] = _dot(x_ref[...].astype(BF16), w_ref[...].astype(BF16)).astype(o_ref.dtype)


def _batched_matmul(x, w, out_dtype, name):
    h, m, k = x.shape
    n = w.shape[2]
    return pl.pallas_call(
        _bmm_kernel, grid=(h,),
        in_specs=[pl.BlockSpec((None, m, k), lambda i: (i, 0, 0)),
                  pl.BlockSpec((None, k, n), lambda i: (i, 0, 0))],
        out_specs=pl.BlockSpec((None, m, n), lambda i: (i, 0, 0)),
        out_shape=jax.ShapeDtypeStruct((h, m, n), out_dtype),
        compiler_params=_cparams(("arbitrary",)), name=name,
    )(x, w)


def _qkv_epilogue(acc, rows, consts):
    nq = N_HEADS * HEAD_DIM
    nk = N_KV_HEADS * HEAD_DIM
    k = acc[:, nq:nq + nk]
    v = acc[:, nq + nk:nq + 2 * nk]
    return acc[:, :nq], k, v, k, v.T


def _qkv_fox_epilogue(acc, rows, consts):
    nq = N_HEADS * HEAD_DIM
    nk = N_KV_HEADS * HEAD_DIM
    (b_f,) = consts
    logf, _ = _log_sigmoid_pair(acc[:, nq + 2 * nk:] + b_f)
    k = acc[:, nq:nq + nk]
    v = acc[:, nq + nk:nq + 2 * nk]
    return acc[:, :nq], k, v, logf, k, v.T


def _mla_down_epilogue(acc, rows, consts):
    cos, sin = rows
    q_norm, kv_norm = consts
    r = q_norm.shape[1]
    c = kv_norm.shape[1]
    cq = _rms_norm(acc[:, :r], q_norm)
    ckv = _rms_norm(acc[:, r:r + c], kv_norm)
    kpe = acc[:, r + c:r + c + LANES] * cos + acc[:, r + c + LANES:] * sin
    return cq, ckv, kpe


def _mla_q_epilogue(acc, rows, consts):
    cos, sin = rows
    n = N_HEADS * HEAD_DIM
    cos_t = jnp.concatenate([cos] * N_HEADS, axis=1)
    sin_t = jnp.concatenate([sin] * N_HEADS, axis=1)
    return acc[:, :n], acc[:, n:2 * n] * cos_t + acc[:, 2 * n:] * sin_t


def _route(x1, rw_hi, rw_lo, rb):
    x_hi, x_lo = _split_bf16(x1)
    logits = _dot_nt(rw_hi, x_hi) + _dot_nt(rw_hi, x_lo) + _dot_nt(rw_lo, x_hi)
    aff = 1.0 / (1.0 + jnp.exp(-logits))
    choice = aff + rb
    c = [choice[e:e + 1, :] for e in range(N_EXPERTS)]
    a = [aff[e:e + 1, :] for e in range(N_EXPERTS)]
    scores = []
    for g in range(N_GROUPS):
        cg = c[g * EXPERTS_PER_GROUP:(g + 1) * EXPERTS_PER_GROUP]
        best = None
        for i in range(EXPERTS_PER_GROUP):
            for j in range(i + 1, EXPERTS_PER_GROUP):
                pair = cg[i] + cg[j]
                best = pair if best is None else jnp.maximum(best, pair)
        scores.append(best)
    grp = jnp.zeros_like(scores[0], dtype=jnp.int32)
    best = scores[0]
    for g in range(1, N_GROUPS):
        better = scores[g] > best
        grp = jnp.where(better, g, grp)
        best = jnp.where(better, scores[g], best)
    w, sel = [], []
    for e in range(N_EXPERTS):
        g = e // EXPERTS_PER_GROUP
        rank = jnp.zeros_like(grp)
        for e2 in range(g * EXPERTS_PER_GROUP, (g + 1) * EXPERTS_PER_GROUP):
            if e2 == e:
                continue
            beats = (c[e2] >= c[e]) if e2 < e else (c[e2] > c[e])
            rank = rank + beats.astype(jnp.int32)
        selected = (grp == g) & (rank < 2)
        sel.append(selected.astype(F32))
        w.append(jnp.where(selected, a[e], 0.0))
    denom = w[0]
    for e in range(1, N_EXPERTS):
        denom = denom + w[e]
    return jnp.concatenate([we / denom for we in w], axis=0), jnp.concatenate(sel, axis=0)


def _oproj_norm_route_kernel(o_ref, w_ref, x_ref, g_ref, b_ref, rwh_ref, rwl_ref, rb_ref, before_ref,
                             x1_ref, gates_ref, sel_ref, rank_ref, counts_ref, carry_ref):
    @pl.when(pl.program_id(0) == 0)
    def _():
        carry_ref[...] = jnp.zeros_like(carry_ref)

    h = _dot(o_ref[...].astype(BF16), w_ref[...])
    x1 = _layer_norm(DN_ALPHA * x_ref[...] + h, g_ref[...], b_ref[...])
    x1_ref[...] = x1
    gates, sel = _route(x1, rwh_ref[...], rwl_ref[...], rb_ref[...])
    gates_ref[...] = gates
    sel_ref[...] = sel
    rank_ref[...] = _dot(sel.astype(BF16), before_ref[...]) + carry_ref[...]
    carry_ref[...] += jnp.sum(sel, axis=1, keepdims=True)
    counts_ref[...] = jnp.broadcast_to(carry_ref[...], counts_ref.shape)


def _oproj_norm_route(o, w_o, x, ln_g, ln_b, rw_hi, rw_lo, rb, tm=256):
    m, d = x.shape
    k = o.shape[1]
    before = _suffix_matrix(tm).T
    const = lambda shape: pl.BlockSpec(shape, lambda i: (0, 0))
    per_token = pl.BlockSpec((N_EXPERTS, tm), lambda i: (0, i))
    per_token_shape = jax.ShapeDtypeStruct((N_EXPERTS, m), F32)
    return pl.pallas_call(
        _oproj_norm_route_kernel, grid=(m // tm,),
        in_specs=[pl.BlockSpec((tm, k), lambda i: (i, 0)),
                  pl.BlockSpec((k, d), lambda i: (0, 0), pipeline_mode=pl.Buffered(1)),
                  pl.BlockSpec((tm, d), lambda i: (i, 0)),
                  const((1, d)), const((1, d)), const(rw_hi.shape), const(rw_lo.shape), const(rb.shape),
                  const(before.shape)],
        out_specs=[pl.BlockSpec((tm, d), lambda i: (i, 0)), per_token, per_token, per_token,
                   const((N_EXPERTS, LANES))],
        out_shape=[jax.ShapeDtypeStruct((m, d), F32), per_token_shape, per_token_shape, per_token_shape,
                   jax.ShapeDtypeStruct((N_EXPERTS, LANES), F32)],
        scratch_shapes=[pltpu.VMEM((N_EXPERTS, 1), F32)],
        compiler_params=_cparams(("arbitrary",)), name="oproj_norm_route",
    )(o, w_o, x, ln_g, ln_b, rw_hi, rw_lo, rb, before)


MOE_SLOT_TILE = 512
MOE_TOKEN_TILE = 256
N_CHOSEN = 2


def _row_copy(src_ref, src_row, dst_ref, dst_row, sem):
    return pltpu.make_async_copy(src_ref.at[pl.ds(src_row, 1), :], dst_ref.at[pl.ds(dst_row, 1), :], sem)


def _moe_dispatch_kernel(slot_ref, x_ref, zeros_ref, xs_ref, sem):
    tm = x_ref.shape[0]

    def issue(t, carry):
        for c in range(N_CHOSEN):
            _row_copy(x_ref, t, xs_ref, slot_ref[c, t], sem).start()
        return carry

    def drain(t, carry):
        for c in range(N_CHOSEN):
            _row_copy(x_ref, 0, xs_ref, 0, sem).wait()
        return carry

    lax.fori_loop(0, tm, issue, 0)
    lax.fori_loop(0, tm, drain, 0)


def _moe_dispatch(x1, slots, n_slots):
    m, d = x1.shape
    tm = MOE_TOKEN_TILE
    return pl.pallas_call(
        _moe_dispatch_kernel, grid=(m // tm,),
        in_specs=[pl.BlockSpec((N_CHOSEN, tm), lambda i: (0, i), memory_space=pltpu.SMEM),
                  pl.BlockSpec((tm, d), lambda i: (i, 0)),
                  pl.BlockSpec(memory_space=pl.ANY)],
        out_specs=pl.BlockSpec(memory_space=pl.ANY),
        out_shape=jax.ShapeDtypeStruct((n_slots, d), F32),
        scratch_shapes=[pltpu.SemaphoreType.DMA(())],
        input_output_aliases={2: 0},
        compiler_params=_cparams(("arbitrary",)), name="moe_dispatch",
    )(slots, x1, jnp.zeros((n_slots, d), F32))


def _moe_ffn_kernel(expert_ref, used_ref, xs_ref, wg_ref, wu_ref, wd_ref, ys_ref, wgb_ref, wub_ref, wdb_ref):
    i = pl.program_id(0)
    prev = expert_ref[jnp.maximum(i - 1, 0)]

    @pl.when(jnp.logical_or(i == 0, expert_ref[i] != prev))
    def _():
        wgb_ref[...] = wg_ref[...].astype(BF16)
        wub_ref[...] = wu_ref[...].astype(BF16)
        wdb_ref[...] = wd_ref[...].astype(BF16)

    @pl.when(i < used_ref[0])
    def _():
        xb = xs_ref[...].astype(BF16)
        hg = _dot(xb, wgb_ref[...])
        hu = _dot(xb, wub_ref[...])
        h = hg / (1.0 + jnp.exp(-hg)) * hu
        ys_ref[...] = _dot(h.astype(BF16), wdb_ref[...])

    @pl.when(i >= used_ref[0])
    def _():
        ys_ref[...] = jnp.zeros_like(ys_ref)


def _moe_ffn(xs, tile_expert, tiles_used, w_gate, w_up, w_down):
    n_slots, d = xs.shape
    f = w_gate.shape[2]
    ts = MOE_SLOT_TILE
    row_map = lambda i, expert, used: (jnp.minimum(i, used[0] - 1), 0)
    w_map = lambda i, expert, used: (expert[i], 0, 0)
    once = pl.Buffered(1)
    return pl.pallas_call(
        _moe_ffn_kernel,
        grid_spec=pltpu.PrefetchScalarGridSpec(
            num_scalar_prefetch=2, grid=(n_slots // ts,),
            in_specs=[pl.BlockSpec((ts, d), row_map),
                      pl.BlockSpec((None, d, f), w_map, pipeline_mode=once),
                      pl.BlockSpec((None, d, f), w_map, pipeline_mode=once),
                      pl.BlockSpec((None, f, d), w_map, pipeline_mode=once)],
            out_specs=pl.BlockSpec((ts, d), lambda i, expert, used: (i, 0)),
            scratch_shapes=[pltpu.VMEM((d, f), BF16), pltpu.VMEM((d, f), BF16), pltpu.VMEM((f, d), BF16)]),
        out_shape=jax.ShapeDtypeStruct((n_slots, d), F32),
        compiler_params=_cparams(("arbitrary",)), name="moe_ffn",
    )(tile_expert, tiles_used, xs, w_gate, w_up, w_down)


def _moe_combine_kernel(slot_ref, x_ref, gate_ref, g_ref, b_ref, ys_ref, o_ref, rows_ref, sem):
    tm = x_ref.shape[0]

    def issue(t, carry):
        for c in range(N_CHOSEN):
            _row_copy(ys_ref, slot_ref[c, t], rows_ref.at[c], t, sem).start()
        return carry

    def drain(t, carry):
        for c in range(N_CHOSEN):
            _row_copy(ys_ref, 0, rows_ref.at[c], 0, sem).wait()
        return carry

    lax.fori_loop(0, tm, issue, 0)
    lax.fori_loop(0, tm, drain, 0)
    gate = gate_ref[...]
    y = gate[:, 0:1] * rows_ref[0] + gate[:, 1:2] * rows_ref[1]
    o_ref[...] = _layer_norm(DN_ALPHA * x_ref[...] + y, g_ref[...], b_ref[...])


def _moe_combine_norm(x1, slots, gates2, ys, ln_g, ln_b):
    m, d = x1.shape
    tm = MOE_TOKEN_TILE
    return pl.pallas_call(
        _moe_combine_kernel, grid=(m // tm,),
        in_specs=[pl.BlockSpec((N_CHOSEN, tm), lambda i: (0, i), memory_space=pltpu.SMEM),
                  pl.BlockSpec((tm, d), lambda i: (i, 0)),
                  pl.BlockSpec((tm, N_CHOSEN), lambda i: (i, 0)),
                  pl.BlockSpec((1, d), lambda i: (0, 0)),
                  pl.BlockSpec((1, d), lambda i: (0, 0)),
                  pl.BlockSpec(memory_space=pl.ANY)],
        out_specs=pl.BlockSpec((tm, d), lambda i: (i, 0)),
        out_shape=jax.ShapeDtypeStruct((m, d), F32),
        scratch_shapes=[pltpu.VMEM((N_CHOSEN, tm, d), F32), pltpu.SemaphoreType.DMA(())],
        compiler_params=_cparams(("arbitrary",)), name="moe_combine_norm",
    )(slots, x1, gates2, ln_g, ln_b, ys)


def _moe_plan(gates_t, sel_t, rank_t, counts):
    m = gates_t.shape[1]
    ts = MOE_SLOT_TILE
    n_tiles = -(-(N_CHOSEN * m + N_EXPERTS * (ts - 1)) // ts)
    cnt = counts[:, 0].astype(jnp.int32)
    padded = (cnt + ts - 1) // ts * ts
    ends = jnp.cumsum(padded)
    starts = ends - padded
    tiles_used = ends[-1] // ts
    tile_ids = jnp.arange(n_tiles, dtype=jnp.int32)
    tile_expert = jnp.sum((tile_ids[:, None] * ts >= ends[None, :]).astype(jnp.int32), axis=1)
    last_expert = jnp.max(jnp.where(tile_ids < tiles_used, tile_expert, 0))
    tile_expert = jnp.where(tile_ids < tiles_used, tile_expert, last_expert)
    chosen = sel_t > 0.0
    slot = starts[:, None] + rank_t.astype(jnp.int32)
    slot_lo = jnp.min(jnp.where(chosen, slot, jnp.iinfo(jnp.int32).max), axis=0)
    slot_hi = jnp.max(jnp.where(chosen, slot, -1), axis=0)
    gate_lo = jnp.sum(jnp.where(chosen & (slot == slot_lo[None, :]), gates_t, 0.0), axis=0)
    gate_hi = jnp.sum(jnp.where(chosen & (slot == slot_hi[None, :]), gates_t, 0.0), axis=0)
    slots = jnp.stack([slot_lo, slot_hi], axis=0).astype(jnp.int32)
    gates2 = jnp.stack([gate_lo, gate_hi], axis=1)
    return slots, gates2, tile_expert.astype(jnp.int32), tiles_used.reshape(1).astype(jnp.int32), n_tiles * ts


def _sparse_moe_norm(x1, gates_t, sel_t, rank_t, counts, w_gate, w_up, w_down, ln_g, ln_b):
    slots, gates2, tile_expert, tiles_used, n_slots = _moe_plan(gates_t, sel_t, rank_t, counts)
    xs = _moe_dispatch(x1, slots, n_slots)
    ys = _moe_ffn(xs, tile_expert, tiles_used, w_gate, w_up, w_down)
    return _moe_combine_norm(x1, slots, gates2, ys, ln_g, ln_b)


PROMPT_TQ = 128
PROMPT_TK = 256
SB_TK = 128
MLA_HEADS_PER_STEP = 8


def _key_positions(j, tk):
    return j * tk + lax.broadcasted_iota(jnp.int32, (tk, 1), 0)


def _query_positions(qi, tq):
    return qi * tq + lax.broadcasted_iota(jnp.int32, (1, tq), 1)


def _head_cols(h):
    return slice(h * HEAD_DIM, (h + 1) * HEAD_DIM)


HEADS_ISSUED_AHEAD = 8


def _issue_ahead(n, produce):
    queue = [produce(h) for h in range(min(HEADS_ISSUED_AHEAD, n))]
    for h in range(n):
        if h + HEADS_ISSUED_AHEAD < n:
            queue.append(produce(h + HEADS_ISSUED_AHEAD))
        yield h, queue.pop(0)


def _softmax_update_t(s, vt, h, m_ref, l_ref, acc_ref):
    m_old = m_ref[h:h + 1, :]
    m_new = jnp.maximum(m_old, jnp.max(s, axis=0, keepdims=True))
    alpha = jnp.exp(m_old - m_new)
    p = jnp.exp(s - m_new)
    l_ref[h:h + 1, :] = alpha * l_ref[h:h + 1, :] + jnp.sum(p, axis=0, keepdims=True)
    acc_ref[h] = alpha * acc_ref[h] + _dot(vt, p.astype(BF16))
    m_ref[h:h + 1, :] = m_new


def _init_softmax_state(m_ref, l_ref, acc_ref):
    m_ref[...] = jnp.full_like(m_ref, MASKED)
    l_ref[...] = jnp.zeros_like(l_ref)
    acc_ref[...] = jnp.zeros_like(acc_ref)


def _store_heads_t(o_ref, acc_ref, l_ref, n_heads):
    for h in range(n_heads):
        o = acc_ref[h] if l_ref is None else acc_ref[h] / l_ref[h:h + 1, :]
        o_ref[:, _head_cols(h)] = o.T.astype(o_ref.dtype)


def _sb_prompt_kernel(q_ref, k_ref, vt_ref, ut_ref, o_ref, qb_ref, run_ref, acc_ref, *, tq, tk):
    qi = pl.program_id(2)
    qb_ref[...] = q_ref[...].astype(BF16)
    run_ref[...] = jnp.zeros_like(run_ref)
    acc_ref[...] = jnp.zeros_like(acc_ref)
    tpos = _query_positions(qi, tq)
    jd = (qi * tq) // tk

    def chunk(j, diagonal):
        start = pl.multiple_of(j * tk, tk)
        k = k_ref[pl.ds(start, tk), :]
        vt = vt_ref[:, pl.ds(start, tk)]
        ut = ut_ref[...]
        if diagonal:
            past = _key_positions(j, tk) < tpos

        def gates(h):
            z = _dot_nt(k, qb_ref[:, _head_cols(h)]) * ATT_SCALE
            ls, lk = _log_sigmoid_pair(z)
            if diagonal:
                lk = jnp.where(past, lk, 0.0)
            hi, lo = _split_bf16(lk)
            later = _dot(ut, hi) + _dot(ut, lo)
            return ls, later, jnp.sum(lk, axis=0, keepdims=True)

        for h, (ls, later, total) in _issue_ahead(KV_GROUP, gates):
            run = run_ref[h:h + 1, :]
            w = jnp.exp(ls + (later + run))
            if diagonal:
                w = jnp.where(past, w, 0.0)
            acc_ref[h] += _dot(vt, w.astype(BF16))
            run_ref[h:h + 1, :] = run + total

    def live():
        return (jnp.max(run_ref[...]) > EXP_UNDERFLOW).astype(jnp.int32)

    chunk(jd, True)

    def cond(c):
        return jnp.logical_and(c[0] >= 0, c[1] > 0)

    def body(c):
        chunk(c[0], False)
        return c[0] - 1, live()

    lax.while_loop(cond, body, (jd - 1, live()))
    _store_heads_t(o_ref, acc_ref, None, KV_GROUP)


def _fox_prompt_kernel(q_ref, k_ref, vt_ref, c_ref, ct_ref, o_ref, qb_ref, m_ref, l_ref, acc_ref, *, tq, tk):
    qi = pl.program_id(2)
    qb_ref[...] = q_ref[...].astype(BF16)
    _init_softmax_state(m_ref, l_ref, acc_ref)
    tpos = _query_positions(qi, tq)
    q0 = pl.multiple_of(qi * tq, tq)
    jd = (qi * tq) // tk

    def chunk(j, diagonal):
        start = pl.multiple_of(j * tk, tk)
        k = k_ref[pl.ds(start, tk), :]
        vt = vt_ref[:, pl.ds(start, tk)]
        ck = c_ref[pl.ds(start, tk), :]
        if diagonal:
            valid = _key_positions(j, tk) <= tpos
        for h, qk in _issue_ahead(KV_GROUP, lambda h: _dot_nt(k, qb_ref[:, _head_cols(h)])):
            cq = ct_ref[h:h + 1, pl.ds(q0, tq)]
            s = qk * ATT_SCALE + cq - ck[:, h:h + 1]
            if diagonal:
                s = jnp.where(valid, s, MASKED)
            _softmax_update_t(s, vt, h, m_ref, l_ref, acc_ref)

    def body(j, carry):
        chunk(j, False)
        return carry

    lax.fori_loop(0, jd, body, 0)
    chunk(jd, True)
    _store_heads_t(o_ref, acc_ref, l_ref, KV_GROUP)


def _moba_select_t(gate, n_blocks_past, n_rows):
    row = lax.broadcasted_iota(jnp.int32, gate.shape, 0)
    is_past = row < n_blocks_past
    gate = jnp.where(is_past, gate, -jnp.inf)
    rank = jnp.zeros(gate.shape, jnp.int32)
    for b2 in range(n_rows):
        g2 = gate[b2:b2 + 1, :]
        beats = (g2 > gate) | ((g2 == gate) & (row > b2))
        rank = rank + beats.astype(jnp.int32)
    return (is_past & (rank < MOBA_TOPK) & (gate > -jnp.inf)).astype(F32)


def _moba_prompt_kernel(q_ref, k_ref, vt_ref, kf_ref, slope_ref, o_ref, qb_ref, kmean_ref, sel_ref,
                        m_ref, l_ref, acc_ref, *, tq, tk, n_blocks):
    qi = pl.program_id(2)

    @pl.when(qi == 0)
    def _():
        kmean_ref[...] = jnp.zeros_like(kmean_ref)
        for b in range(n_blocks):
            blk = kf_ref[b * MOBA_BLOCK:(b + 1) * MOBA_BLOCK, :]
            kmean_ref[b:b + 1, :] = jnp.sum(blk, axis=0, keepdims=True) * (1.0 / MOBA_BLOCK)

    qb_ref[...] = q_ref[...].astype(BF16)
    _init_softmax_state(m_ref, l_ref, acc_ref)
    tpos = _query_positions(qi, tq)
    own = (qi * tq) // MOBA_BLOCK
    kmean = kmean_ref[...]
    for h in range(KV_GROUP):
        sel_ref[h] = _moba_select_t(_dot_nt_precise(kmean, q_ref[:, _head_cols(h)]), own, n_blocks)

    def chunk(j, diagonal):
        start = pl.multiple_of(j * tk, tk)
        k = k_ref[pl.ds(start, tk), :]
        vt = vt_ref[:, pl.ds(start, tk)]
        kpos = _key_positions(j, tk)
        kdist = (kpos - tpos).astype(F32)
        if diagonal:
            valid = kpos <= tpos
        for h, qk in _issue_ahead(KV_GROUP, lambda h: _dot_nt(k, qb_ref[:, _head_cols(h)])):
            s = qk * ATT_SCALE + slope_ref[h:h + 1, :] * kdist
            if not diagonal:
                valid = sel_ref[h, pl.ds(j, 1), :] > 0.0
            s = jnp.where(valid, s, MASKED)
            _softmax_update_t(s, vt, h, m_ref, l_ref, acc_ref)

    def body(j, carry):
        chunk(j, False)
        return carry

    lax.fori_loop(0, own, body, 0)
    chunk(own, True)
    _store_heads_t(o_ref, acc_ref, l_ref, KV_GROUP)


def _gqa_prompt_attention(kind, q, kb, vt, batch, seq, extra=()):
    tq = PROMPT_TQ
    tk = {"sb": SB_TK, "fox": PROMPT_TK, "moba": MOBA_BLOCK}[kind]
    nq = seq // tq
    gw = KV_GROUP * HEAD_DIM
    q_spec = pl.BlockSpec((tq, gw), lambda b, h, i: (b * nq + i, h))
    k_spec = pl.BlockSpec((seq, HEAD_DIM), lambda b, h, i: (b, h))
    vt_spec = pl.BlockSpec((HEAD_DIM, seq), lambda b, h, i: (h, b))
    qb = pltpu.VMEM((tq, gw), BF16)
    row = pltpu.VMEM((KV_GROUP, tq), F32)
    acc = pltpu.VMEM((KV_GROUP, HEAD_DIM, tq), F32)
    if kind == "sb":
        (ut,) = extra
        kern = functools.partial(_sb_prompt_kernel, tq=tq, tk=tk)
        in_specs = [q_spec, k_spec, vt_spec, pl.BlockSpec(ut.shape, lambda b, h, i: (0, 0))]
        scratch = [qb, row, acc]
    elif kind == "fox":
        c, ct = extra
        kern = functools.partial(_fox_prompt_kernel, tq=tq, tk=tk)
        in_specs = [q_spec, k_spec, vt_spec,
                    pl.BlockSpec((None, None, seq, KV_GROUP), lambda b, h, i: (b, h, 0, 0)),
                    pl.BlockSpec((None, None, KV_GROUP, seq), lambda b, h, i: (b, h, 0, 0))]
        scratch = [qb, row, row, acc]
    else:
        kf, slopes = extra
        n_blocks = seq // MOBA_BLOCK
        nb_pad = -(-n_blocks // SUBLANES) * SUBLANES
        kern = functools.partial(_moba_prompt_kernel, tq=tq, tk=tk, n_blocks=n_blocks)
        in_specs = [q_spec, k_spec, vt_spec, k_spec,
                    pl.BlockSpec((None, KV_GROUP, tq), lambda b, h, i: (h, 0, 0))]
        scratch = [qb, pltpu.VMEM((nb_pad, HEAD_DIM), F32), pltpu.VMEM((KV_GROUP, nb_pad, tq), F32),
                   row, row, acc]
    return pl.pallas_call(
        kern, grid=(batch, N_KV_HEADS, nq), in_specs=in_specs, out_specs=q_spec,
        out_shape=jax.ShapeDtypeStruct((batch * seq, N_HEADS * HEAD_DIM), BF16),
        scratch_shapes=scratch,
        compiler_params=_cparams(("arbitrary", "arbitrary", "arbitrary")), name=kind + "_prompt",
    )(q, kb, vt, *extra)


def _mla_kv_up_kernel(c_ref, wk_ref, wvt_ref, kn_ref, vt_ref):
    c = c_ref[...].astype(BF16)
    kn_ref[...] = _dot(c, wk_ref[...]).astype(BF16)
    vt_ref[...] = _dot_nt(wvt_ref[...], c).astype(BF16)


def _mla_kv_up(ckv, w_uk, w_uv_t, tm=256):
    m, r = ckv.shape
    n = w_uk.shape[1]
    const = pl.Buffered(1)
    return pl.pallas_call(
        _mla_kv_up_kernel, grid=(m // tm,),
        in_specs=[pl.BlockSpec((tm, r), lambda i: (i, 0)),
                  pl.BlockSpec((r, n), lambda i: (0, 0), pipeline_mode=const),
                  pl.BlockSpec((n, r), lambda i: (0, 0), pipeline_mode=const)],
        out_specs=[pl.BlockSpec((tm, n), lambda i: (i, 0)), pl.BlockSpec((n, tm), lambda i: (0, i))],
        out_shape=[jax.ShapeDtypeStruct((m, n), BF16), jax.ShapeDtypeStruct((n, m), BF16)],
        compiler_params=_cparams(("arbitrary",)), name="mla_kv_up",
    )(ckv, w_uk, w_uv_t)


def _mla_prompt_kernel(qn_ref, qr_ref, kn_ref, vt_ref, kr_ref, o_ref, m_ref, l_ref, acc_ref, *, tq, tk, nh):
    qi = pl.program_id(2)
    _init_softmax_state(m_ref, l_ref, acc_ref)
    tpos = _query_positions(qi, tq)
    jd = (qi * tq) // tk

    def chunk(j, diagonal):
        start = pl.multiple_of(j * tk, tk)
        kr = kr_ref[pl.ds(start, tk), :].astype(BF16)
        if diagonal:
            valid = _key_positions(j, tk) <= tpos

        def scores(h):
            kn = kn_ref[pl.ds(start, tk), _head_cols(h)]
            return _dot_nt(kn, qn_ref[:, _head_cols(h)]) + _dot_nt(kr, qr_ref[:, _head_cols(h)])

        for h, qk in _issue_ahead(nh, scores):
            vt = vt_ref[_head_cols(h), pl.ds(start, tk)]
            s = qk * MLA_SCALE
            if diagonal:
                s = jnp.where(valid, s, MASKED)
            _softmax_update_t(s, vt, h, m_ref, l_ref, acc_ref)

    def body(j, carry):
        chunk(j, False)
        return carry

    lax.fori_loop(0, jd, body, 0)
    chunk(jd, True)
    _store_heads_t(o_ref, acc_ref, l_ref, nh)


def _mla_prompt_attention(q_nope, q_pe, k_nope, vt, k_pe, batch, seq):
    tq, tk, nh = PROMPT_TQ, PROMPT_TK, MLA_HEADS_PER_STEP
    nq = seq // tq
    gw = nh * HEAD_DIM
    q_spec = pl.BlockSpec((tq, gw), lambda b, h, i: (b * nq + i, h))
    row = pltpu.VMEM((nh, tq), F32)
    return pl.pallas_call(
        functools.partial(_mla_prompt_kernel, tq=tq, tk=tk, nh=nh), grid=(batch, N_HEADS // nh, nq),
        in_specs=[q_spec, q_spec,
                  pl.BlockSpec((seq, gw), lambda b, h, i: (b, h)),
                  pl.BlockSpec((gw, seq), lambda b, h, i: (h, b)),
                  pl.BlockSpec((seq, LANES), lambda b, h, i: (b, 0))],
        out_specs=q_spec,
        out_shape=jax.ShapeDtypeStruct((batch * seq, N_HEADS * MLA_V), BF16),
        scratch_shapes=[row, row, pltpu.VMEM((nh, MLA_V, tq), F32)],
        compiler_params=_cparams(("arbitrary", "arbitrary", "arbitrary")), name="mla_prompt",
    )(q_nope, q_pe, k_nope, vt, k_pe)


PAGES_PER_STEP = 16
SB_HEAD_PAGES = 4
SB_TAIL_PAGES_PER_STEP = 12


def _page_specs(block, layer, n_pages, pages_per_step, first_slot):
    def make(p):
        def index_map(s, g, *prefetch):
            return (layer, prefetch[0][s * n_pages + first_slot(g) + p]) + (0,) * len(block)
        return pl.BlockSpec((None, None) + block, index_map)
    return [make(p) for p in range(pages_per_step)]


def _newest_first(n_groups, pages_per_step):
    return lambda g: (n_groups - 1 - g) * pages_per_step


def _kv_page_head(ref, kvh):
    return ref[pl.ds(kvh, PAGE_SIZE, stride=N_KV_HEADS), :].astype(BF16)


def _gather_kv(page_refs, kvh):
    return jnp.concatenate([_kv_page_head(r, kvh) for r in page_refs], axis=0)


def _sample_row_query(n_q):
    r = lax.broadcasted_iota(jnp.int32, (n_q * KV_GROUP, 1), 0)
    return r // KV_GROUP


def _sb_sample_chunk(kvh, q, k, v, past, u, run_ref, acc_ref):
    sub = u.shape[0]
    z = _dot_nt(q, k) * ATT_SCALE
    ls, lnk = _log_sigmoid_pair(z)
    for sb in range(k.shape[0] // sub - 1, -1, -1):
        cols = slice(sb * sub, (sb + 1) * sub)
        lk = lnk[:, cols] if past is None else jnp.where(past[:, cols], lnk[:, cols], 0.0)
        between = _suffix_exclusive(lk, u) + run_ref[kvh]
        w = jnp.exp(ls[:, cols] + between)
        if past is not None:
            w = jnp.where(past[:, cols], w, 0.0)
        acc_ref[kvh] += _dot(w.astype(BF16), v[cols, :])
        run_ref[kvh] += jnp.sum(lk, axis=1, keepdims=True)


def _sb_sample_head_kernel(pt_ref, q_ref, kn_ref, vn_ref, u_ref, *rest, n_q):
    k_pages = rest[:SB_HEAD_PAGES]
    v_pages = rest[SB_HEAD_PAGES:2 * SB_HEAD_PAGES]
    acc_out, run_out, run_ref, acc_ref = rest[2 * SB_HEAD_PAGES:]
    qidx = _sample_row_query(n_q)
    run_ref[...] = jnp.zeros_like(run_ref)
    acc_ref[...] = jnp.zeros_like(acc_ref)
    lane = lax.broadcasted_iota(jnp.int32, (1, PAGE_SIZE), 1)
    past = lane < qidx
    u_page = u_ref[:PAGE_SIZE, :PAGE_SIZE]
    for kvh in range(N_KV_HEADS):
        q = q_ref[kvh].astype(BF16)
        _sb_sample_chunk(kvh, q, _kv_page_head(kn_ref, kvh), _kv_page_head(vn_ref, kvh), past, u_page,
                         run_ref, acc_ref)
        _sb_sample_chunk(kvh, q, _gather_kv(k_pages, kvh), _gather_kv(v_pages, kvh), None, u_ref[...],
                         run_ref, acc_ref)
    acc_out[...] = acc_ref[...]
    run_out[...] = jnp.broadcast_to(run_ref[...], run_out.shape)


def _sb_sample_tail_kernel(pt_ref, done_ref, q_ref, acc_in, run_in, u_ref, *rest):
    n = SB_TAIL_PAGES_PER_STEP
    k_pages = rest[:n]
    v_pages = rest[n:2 * n]
    o_ref, run_ref, acc_ref = rest[2 * n:]
    s = pl.program_id(0)
    g = pl.program_id(1)

    @pl.when(g == 0)
    def _():
        run_ref[...] = run_in[...][:, :, :1]
        acc_ref[...] = acc_in[...]

    @pl.when(jnp.logical_and(done_ref[s] == 0, jnp.max(run_ref[...]) > EXP_UNDERFLOW))
    def _():
        for kvh in range(N_KV_HEADS):
            q = q_ref[kvh].astype(BF16)
            _sb_sample_chunk(kvh, q, _gather_kv(k_pages, kvh), _gather_kv(v_pages, kvh), None, u_ref[...],
                             run_ref, acc_ref)

    @pl.when(g == pl.num_programs(1) - 1)
    def _():
        o_ref[...] = acc_ref[...].astype(o_ref.dtype)


def _alibi_slopes_rows_sample(kvh, n_q):
    r = lax.broadcasted_iota(jnp.int32, (n_q * KV_GROUP, 1), 0)
    h = kvh * KV_GROUP + r % KV_GROUP
    return jnp.exp2(-8.0 * (h + 1).astype(F32) / N_HEADS)


def _softmax_sample_kernel(pt_ref, q_ref, kn_ref, vn_ref, *rest, n_q, kind, n_groups):
    if kind == "fox":
        fnew_ref = rest[0]
        f_pages = rest[1:1 + PAGES_PER_STEP]
        rest = rest[1 + PAGES_PER_STEP:]
    else:
        sel_ref = rest[0]
        rest = rest[1:]
    k_pages = rest[:PAGES_PER_STEP]
    v_pages = rest[PAGES_PER_STEP:2 * PAGES_PER_STEP]
    o_ref, m_ref, l_ref, acc_ref = rest[2 * PAGES_PER_STEP:2 * PAGES_PER_STEP + 4]
    fcarry_ref = rest[2 * PAGES_PER_STEP + 4] if kind == "fox" else None
    g = pl.program_id(1)
    rows = n_q * KV_GROUP
    qidx = _sample_row_query(n_q)
    chunk_keys = PAGES_PER_STEP * PAGE_SIZE
    past_len = n_groups * chunk_keys
    first_slot = (n_groups - 1 - g) * PAGES_PER_STEP

    @pl.when(g == 0)
    def _():
        m_ref[...] = jnp.full_like(m_ref, MASKED)
        l_ref[...] = jnp.zeros_like(l_ref)
        acc_ref[...] = jnp.zeros_like(acc_ref)
        lane = lax.broadcasted_iota(jnp.int32, (1, PAGE_SIZE), 1)
        valid = lane <= qidx
        if kind == "fox":
            fcarry_ref[...] = jnp.zeros_like(fcarry_ref)
        for kvh in range(N_KV_HEADS):
            q = q_ref[kvh].astype(BF16)
            s = _dot_nt(q, _kv_page_head(kn_ref, kvh)) * ATT_SCALE
            if kind == "fox":
                f = fnew_ref[kvh]
                bias = jnp.zeros((rows, PAGE_SIZE), F32)
                for j in range(n_q):
                    between = (lane > j) & (lane <= qidx)
                    bj = jnp.sum(jnp.where(between, f, 0.0), axis=1, keepdims=True)
                    bias = jnp.where(lane == j, bj, bias)
                s = s + bias
            else:
                slope = _alibi_slopes_rows_sample(kvh, n_q)
                s = s - slope * (qidx - lane).astype(F32)
            s = jnp.where(valid, s, MASKED)
            _softmax_update(s, valid, _kv_page_head(vn_ref, kvh), m_ref.at[kvh], l_ref.at[kvh], acc_ref.at[kvh])

    if kind == "fox":
        f = jnp.concatenate([r[...] for r in f_pages], axis=1)
        lane_c = lax.broadcasted_iota(jnp.int32, f.shape, 1)
        incl = f
        d = 1
        while d < chunk_keys:
            shifted = pltpu.roll(incl, chunk_keys - d, axis=1)
            incl = incl + jnp.where(lane_c + d < chunk_keys, shifted, 0.0)
            d *= 2
        suffix = incl - f + fcarry_ref[...]
        lane = lax.broadcasted_iota(jnp.int32, (1, PAGE_SIZE), 1)

    qk = [_dot_nt(q_ref[kvh].astype(BF16), _gather_kv(k_pages, kvh)) for kvh in range(N_KV_HEADS)]
    for kvh in range(N_KV_HEADS):
        s = qk[kvh] * ATT_SCALE
        if kind == "fox":
            f_new = fnew_ref[kvh]
            upto_q = jnp.sum(jnp.where(lane <= qidx, f_new, 0.0), axis=1, keepdims=True)
            sfx = suffix[kvh * KV_GROUP:(kvh + 1) * KV_GROUP, :]
            s = s + jnp.concatenate([sfx] * n_q, axis=0) + upto_q
            valid = None
        else:
            slope = _alibi_slopes_rows_sample(kvh, n_q)
            kpos = first_slot * PAGE_SIZE + lax.broadcasted_iota(jnp.int32, (1, chunk_keys), 1)
            s = s - slope * (past_len + qidx - kpos).astype(F32)
            sel = sel_ref[kvh]
            sel_lane = lax.broadcasted_iota(jnp.int32, sel.shape, 1)
            pages_per_block = MOBA_BLOCK // PAGE_SIZE
            cols = []
            for bb in range(PAGES_PER_STEP // pages_per_block):
                blk = first_slot // pages_per_block + bb
                picked = jnp.sum(jnp.where(sel_lane == blk, sel, 0.0), axis=1, keepdims=True) > 0.0
                cols.append(jnp.broadcast_to(picked, (rows, MOBA_BLOCK)))
            valid = jnp.concatenate(cols, axis=1)
            s = jnp.where(valid, s, MASKED)
        _softmax_update(s, valid, _gather_kv(v_pages, kvh), m_ref.at[kvh], l_ref.at[kvh], acc_ref.at[kvh])

    if kind == "fox":
        fcarry_ref[...] += jnp.sum(f, axis=1, keepdims=True)

    @pl.when(g == pl.num_programs(1) - 1)
    def _():
        o_ref[...] = (acc_ref[...] / l_ref[...]).astype(o_ref.dtype)


def _moba_select(gate, n_blocks_past, n_cols):
    lane = lax.broadcasted_iota(jnp.int32, gate.shape, 1)
    is_past = lane < n_blocks_past
    gate = jnp.where(is_past, gate, -jnp.inf)
    rank = jnp.zeros(gate.shape, jnp.int32)
    for b2 in range(n_cols):
        g2 = gate[:, b2:b2 + 1]
        beats = (g2 > gate) | ((g2 == gate) & (lane > b2))
        rank = rank + beats.astype(jnp.int32)
    return (is_past & (rank < MOBA_TOPK) & (gate > -jnp.inf)).astype(F32)


def _moba_sample_select_kernel(pt_ref, q_ref, *rest, n_groups):
    k_pages = rest[:PAGES_PER_STEP]
    sel_ref, kmean_ref = rest[PAGES_PER_STEP:]
    g = pl.program_id(1)
    pages_per_block = MOBA_BLOCK // PAGE_SIZE
    blocks_per_step = PAGES_PER_STEP // pages_per_block

    @pl.when(g == 0)
    def _():
        kmean_ref[...] = jnp.zeros_like(kmean_ref)

    first_block = pl.multiple_of((n_groups - 1 - g) * blocks_per_step, blocks_per_step)
    for kvh in range(N_KV_HEADS):
        means = []
        for bb in range(blocks_per_step):
            tot = None
            for p in range(pages_per_block):
                page = k_pages[bb * pages_per_block + p][pl.ds(kvh, PAGE_SIZE, stride=N_KV_HEADS), :]
                part = jnp.sum(page, axis=0, keepdims=True)
                tot = part if tot is None else tot + part
            means.append(tot * (1.0 / MOBA_BLOCK))
        kmean_ref[kvh, pl.ds(first_block, blocks_per_step), :] = jnp.concatenate(means, axis=0)

    @pl.when(g == n_groups - 1)
    def _():
        n_blocks = n_groups * blocks_per_step
        for kvh in range(N_KV_HEADS):
            gate = _dot_nt_precise(q_ref[kvh], kmean_ref[kvh])
            sel_ref[kvh] = _moba_select(gate, n_blocks, n_blocks)


def _sample_call(kern, name, prefetch, grid, fixed_ins, fixed_specs, paged_ins, paged_specs,
                 out_shape, out_spec, scratch):
    ins = list(fixed_ins)
    specs = list(fixed_specs)
    for arr, sp in zip(paged_ins, paged_specs):
        ins += [arr] * len(sp)
        specs += sp
    return pl.pallas_call(
        kern,
        grid_spec=pltpu.PrefetchScalarGridSpec(
            num_scalar_prefetch=len(prefetch), grid=grid, in_specs=specs, out_specs=out_spec,
            scratch_shapes=scratch),
        out_shape=out_shape,
        compiler_params=_cparams(("arbitrary", "arbitrary")), name=name,
    )(*prefetch, *ins)


def _seq_spec(shape):
    nd = len(shape)
    return pl.BlockSpec((None,) + shape, lambda s, g, *prefetch: (s,) + (0,) * nd)


def _const_spec(shape):
    nd = len(shape)
    return pl.BlockSpec(shape, lambda s, g, *prefetch: (0,) * nd)


def _sb_sample_attention(page_table, q_rows, k_new_page, v_new_page, cache_k, cache_v, layer, u):
    n_seq, _, rows, _ = q_rows.shape
    n_q = rows // KV_GROUP
    n_pages = page_table.shape[1]
    n_tail = n_pages - SB_HEAD_PAGES
    assert n_tail % SB_TAIL_PAGES_PER_STEP == 0
    tail_groups = n_tail // SB_TAIL_PAGES_PER_STEP
    kv_block = (PAGE_SIZE * N_KV_HEADS, HEAD_DIM)
    q_spec = _seq_spec((N_KV_HEADS, rows, HEAD_DIM))
    new_spec = _seq_spec(kv_block)
    state_shape = jax.ShapeDtypeStruct((n_seq, N_KV_HEADS, rows, HEAD_DIM), F32)
    col = pltpu.VMEM((N_KV_HEADS, rows, 1), F32)
    acc = pltpu.VMEM((N_KV_HEADS, rows, HEAD_DIM), F32)
    head_specs = _page_specs(kv_block, layer, n_pages, SB_HEAD_PAGES, lambda g: n_tail)
    acc0, run0 = _sample_call(
        functools.partial(_sb_sample_head_kernel, n_q=n_q), "sb_sample_head", [page_table.reshape(-1)],
        (n_seq, 1), [q_rows, k_new_page, v_new_page, u], [q_spec, new_spec, new_spec, _const_spec(u.shape)],
        [cache_k, cache_v], [head_specs, head_specs], [state_shape, state_shape], [q_spec, q_spec], [col, acc])
    done = jnp.max(run0, axis=(1, 2, 3)) < EXP_UNDERFLOW
    pt_tail = jnp.where(done[:, None], page_table[0, 0], page_table).reshape(-1)
    tail_specs = _page_specs(kv_block, layer, n_pages, SB_TAIL_PAGES_PER_STEP,
                             _newest_first(tail_groups, SB_TAIL_PAGES_PER_STEP))
    return _sample_call(
        _sb_sample_tail_kernel, "sb_sample_tail", [pt_tail, done.astype(jnp.int32)], (n_seq, tail_groups),
        [q_rows, acc0, run0, u], [q_spec, q_spec, q_spec, _const_spec(u.shape)],
        [cache_k, cache_v], [tail_specs, tail_specs],
        jax.ShapeDtypeStruct((n_seq, N_KV_HEADS, rows, HEAD_DIM), BF16), q_spec, [col, acc])


def _gqa_sample_attention(kind, page_table, q_rows, k_new_page, v_new_page, cache_k, cache_v, layer,
                          extra=None):
    n_seq, _, rows, _ = q_rows.shape
    n_q = rows // KV_GROUP
    n_pages = page_table.shape[1]
    n_groups = n_pages // PAGES_PER_STEP
    grid = (n_seq, n_groups)
    prefetch = [page_table.reshape(-1)]
    first_slot = _newest_first(n_groups, PAGES_PER_STEP)
    kv_block = (PAGE_SIZE * N_KV_HEADS, HEAD_DIM)
    kv_specs = _page_specs(kv_block, layer, n_pages, PAGES_PER_STEP, first_slot)
    q_spec = _seq_spec((N_KV_HEADS, rows, HEAD_DIM))
    new_spec = _seq_spec(kv_block)
    out_shape = jax.ShapeDtypeStruct((n_seq, N_KV_HEADS, rows, HEAD_DIM), BF16)
    col = pltpu.VMEM((N_KV_HEADS, rows, 1), F32)
    acc = pltpu.VMEM((N_KV_HEADS, rows, HEAD_DIM), F32)
    kern = functools.partial(_softmax_sample_kernel, n_q=n_q, kind=kind, n_groups=n_groups)
    if kind == "fox":
        f_new_rows, cache_logf_t = extra
        f_specs = _page_specs((N_HEADS, PAGE_SIZE), layer, n_pages, PAGES_PER_STEP, first_slot)
        return _sample_call(
            kern, "fox_sample", prefetch, grid,
            [q_rows, k_new_page, v_new_page, f_new_rows],
            [q_spec, new_spec, new_spec, _seq_spec((N_KV_HEADS, rows, LANES))],
            [cache_logf_t, cache_k, cache_v], [f_specs, kv_specs, kv_specs], out_shape, q_spec,
            [col, col, acc, pltpu.VMEM((N_HEADS, 1), F32)])
    sel = _sample_call(
        functools.partial(_moba_sample_select_kernel, n_groups=n_groups), "moba_sample_select",
        prefetch, grid, [q_rows], [q_spec], [cache_k], [kv_specs],
        jax.ShapeDtypeStruct((n_seq, N_KV_HEADS, rows, LANES), F32),
        _seq_spec((N_KV_HEADS, rows, LANES)), [pltpu.VMEM((N_KV_HEADS, LANES, HEAD_DIM), F32)])
    return _sample_call(
        kern, "moba_sample", prefetch, grid,
        [q_rows, k_new_page, v_new_page, sel],
        [q_spec, new_spec, new_spec, _seq_spec((N_KV_HEADS, rows, LANES))],
        [cache_k, cache_v], [kv_specs, kv_specs], out_shape, q_spec, [col, col, acc])


def _mla_sample_kernel(pt_ref, ql_ref, qr_ref, cn_ref, rn_ref, *rest, n_q):
    c_pages = rest[:PAGES_PER_STEP]
    r_pages = rest[PAGES_PER_STEP:2 * PAGES_PER_STEP]
    o_ref, m_ref, l_ref, acc_ref = rest[2 * PAGES_PER_STEP:]
    g = pl.program_id(1)
    ql = ql_ref[...]
    qr = qr_ref[...][:, :MLA_ROPE]
    r = lax.broadcasted_iota(jnp.int32, (ql.shape[0], 1), 0)
    qidx = r % n_q

    @pl.when(g == 0)
    def _():
        m_ref[...] = jnp.full_like(m_ref, MASKED)
        l_ref[...] = jnp.zeros_like(l_ref)
        acc_ref[...] = jnp.zeros_like(acc_ref)
        c = cn_ref[...].astype(BF16)
        s = (_dot_nt(ql, c) + _dot_nt(qr, rn_ref[...].astype(BF16))) * MLA_SCALE
        lane = lax.broadcasted_iota(jnp.int32, (1, PAGE_SIZE), 1)
        valid = lane <= qidx
        s = jnp.where(valid, s, MASKED)
        _softmax_update(s, valid, c, m_ref, l_ref, acc_ref)

    c = jnp.concatenate([p[...].astype(BF16) for p in c_pages], axis=0)
    kr = jnp.concatenate([p[...].astype(BF16) for p in r_pages], axis=0)
    s = (_dot_nt(ql, c) + _dot_nt(qr, kr)) * MLA_SCALE
    _softmax_update(s, None, c, m_ref, l_ref, acc_ref)

    @pl.when(g == pl.num_programs(1) - 1)
    def _():
        o_ref[...] = (acc_ref[...] / l_ref[...]).astype(o_ref.dtype)


def _mla_sample_attention(page_table, q_lat, q_pe, c_new_page, r_new_page, cache_ckv, cache_kpe, layer):
    n_seq, rows, rank = q_lat.shape
    n_q = rows // N_HEADS
    n_pages = page_table.shape[1]
    n_groups = n_pages // PAGES_PER_STEP
    first_slot = _newest_first(n_groups, PAGES_PER_STEP)
    c_block = (PAGE_SIZE, rank)
    r_block = (PAGE_SIZE, MLA_ROPE)
    return _sample_call(
        functools.partial(_mla_sample_kernel, n_q=n_q), "mla_sample", [page_table.reshape(-1)],
        (n_seq, n_groups), [q_lat, q_pe, c_new_page, r_new_page],
        [_seq_spec((rows, rank)), _seq_spec((rows, LANES)), _seq_spec(c_block), _seq_spec(r_block)],
        [cache_ckv, cache_kpe],
        [_page_specs(c_block, layer, n_pages, PAGES_PER_STEP, first_slot),
         _page_specs(r_block, layer, n_pages, PAGES_PER_STEP, first_slot)],
        jax.ShapeDtypeStruct((n_seq, rows, rank), BF16), _seq_spec((rows, rank)),
        [pltpu.VMEM((rows, 1), F32), pltpu.VMEM((rows, 1), F32), pltpu.VMEM((rows, rank), F32)])


def _cumsum_lanes_kernel(x_ref, o_ref):
    x = x_ref[...]
    n = x.shape[1]
    lane = lax.broadcasted_iota(jnp.int32, x.shape, 1)
    d = 1
    while d < n:
        x = x + jnp.where(lane >= d, pltpu.roll(x, d, axis=1), 0.0)
        d *= 2
    o_ref[...] = x


def _cumsum_lanes(x):
    b, r, t = x.shape
    return pl.pallas_call(
        _cumsum_lanes_kernel, grid=(b,),
        in_specs=[pl.BlockSpec((None, r, t), lambda i: (i, 0, 0))],
        out_specs=pl.BlockSpec((None, r, t), lambda i: (i, 0, 0)),
        out_shape=jax.ShapeDtypeStruct(x.shape, F32),
        compiler_params=_cparams(("arbitrary",)), name="cumsum_lanes",
    )(x)


def _sample_rows(x, n_seq, n_q):
    x = x.reshape(n_seq, n_q, N_KV_HEADS, KV_GROUP, HEAD_DIM)
    return jnp.transpose(x, (0, 2, 1, 3, 4)).reshape(n_seq, N_KV_HEADS, n_q * KV_GROUP, HEAD_DIM)


def _sample_rows_back(o, n_seq, n_q):
    o = o.reshape(n_seq, N_KV_HEADS, n_q, KV_GROUP, HEAD_DIM)
    return jnp.transpose(o, (0, 2, 1, 3, 4)).reshape(n_seq * n_q, N_HEADS * HEAD_DIM)


def _new_page(rows, n_seq, n_q):
    w = rows.shape[1]
    per_tok = w // HEAD_DIM
    page = rows.reshape(n_seq, n_q * per_tok, HEAD_DIM)
    return jnp.pad(page, ((0, 0), (0, (PAGE_SIZE - n_q) * per_tok), (0, 0)))


def _suffix_matrix(n):
    j = lax.broadcasted_iota(jnp.int32, (n, n), 0)
    s = lax.broadcasted_iota(jnp.int32, (n, n), 1)
    return (j > s).astype(BF16)


def _rope_tables(pos):
    half = MLA_ROPE // 2
    inv = ROPE_THETA ** (-jnp.arange(half, dtype=F32) / half)
    ang = pos.astype(F32)[:, None] * inv[None, :]
    zeros = jnp.zeros((pos.shape[0], LANES - MLA_ROPE), F32)
    cos, sin = jnp.cos(ang), jnp.sin(ang)
    return jnp.concatenate([cos, cos, zeros], 1), jnp.concatenate([sin, sin, zeros], 1)


def _rotate_half_weights(w):
    half = MLA_ROPE // 2
    x1, x2 = w[..., :half], w[..., half:]
    pad = jnp.zeros(w.shape[:-1] + (LANES - MLA_ROPE,), w.dtype)
    return jnp.concatenate([x1, x2, pad], -1), jnp.concatenate([-x2, x1, pad], -1)


def kernel(x_prompt, x_sample, cache_sb_k, cache_sb_v, cache_mla_ckv, cache_mla_kpe, cache_fox_k, cache_fox_v, cache_fox_logf, cache_moba_k, cache_moba_v, page_table, ln_g, ln_b, sb_w_qkv, sb_w_o, mla_w_dq, mla_q_norm, mla_w_uq, mla_w_dkv, mla_kv_norm, mla_w_ukv, mla_w_o, fox_w_qkv, fox_w_f, fox_b_f, fox_w_o, moba_w_qkv, moba_w_o, router_w, router_b, moe_w_gate, moe_w_up, moe_w_down):
    batch, seq, d_model = x_prompt.shape
    n_seq, n_q, _ = x_sample.shape
    n_pages = page_table.shape[1]
    past_len = n_pages * PAGE_SIZE
    mp = batch * seq
    ms = n_seq * n_q
    m = mp + ms
    nq_w = N_HEADS * HEAD_DIM
    nk_w = N_KV_HEADS * HEAD_DIM

    x = jnp.concatenate([x_prompt.reshape(mp, d_model), x_sample.reshape(ms, d_model)], axis=0)
    u = _suffix_matrix(MOBA_BLOCK)
    rw_hi, rw_lo = _split_bf16(router_w.T)
    rb = router_b.reshape(N_EXPERTS, 1).astype(F32)
    pos = jnp.concatenate([jnp.tile(jnp.arange(seq, dtype=jnp.int32), batch),
                           jnp.tile(past_len + jnp.arange(n_q, dtype=jnp.int32), n_seq)])
    cos_t, sin_t = _rope_tables(pos)
    alibi = jnp.exp2(-8.0 * jnp.arange(1, N_HEADS + 1, dtype=F32) / N_HEADS)
    alibi_rows = jnp.broadcast_to(alibi.reshape(N_KV_HEADS, KV_GROUP, 1), (N_KV_HEADS, KV_GROUP, PROMPT_TQ))

    def kv_pages(cache):
        return cache.reshape(cache.shape[0], cache.shape[1], PAGE_SIZE * N_KV_HEADS, HEAD_DIM)

    def kv_outputs(k, v):
        kp = k[:mp].reshape(1, batch, seq, N_KV_HEADS, HEAD_DIM)
        vp = v[:mp].reshape(1, batch, seq, N_KV_HEADS, HEAD_DIM)
        ks = k[mp:].reshape(1, n_seq, n_q, N_KV_HEADS, HEAD_DIM)
        vs = v[mp:].reshape(1, n_seq, n_q, N_KV_HEADS, HEAD_DIM)
        return kp, vp, ks, vs

    def gqa_layer(kind, w_qkv, cache_k, cache_v, w_f=None, b_f=None, cache_logf=None):
        outs = [(nq_w, F32), (nk_w, F32), (nk_w, F32)]
        t_outs = [(nk_w, BF16)]
        if kind == "fox":
            w = jnp.concatenate([w_qkv, jnp.pad(w_f, ((0, 0), (0, LANES - N_HEADS)))], axis=1).astype(BF16)
            bf = jnp.pad(b_f.astype(F32), (0, LANES - N_HEADS)).reshape(1, LANES)
            q, k, v, logf, kb, vt = _fused_matmul(
                x, w, _qkv_fox_epilogue, outs + [(LANES, F32), (nk_w, BF16)], const_ins=(bf,),
                t_outs=t_outs, name="qkv_fox")
            logf = logf[:, :N_HEADS]
        else:
            q, k, v, kb, vt = _fused_matmul(x, w_qkv.astype(BF16), _qkv_epilogue, outs + [(nk_w, BF16)],
                                            t_outs=t_outs, name="qkv_" + kind)
        q_rows = _sample_rows(q[mp:], n_seq, n_q)
        k_new = _new_page(k[mp:], n_seq, n_q)
        v_new = _new_page(v[mp:], n_seq, n_q)
        if kind == "sb":
            op = _gqa_prompt_attention("sb", q, kb, vt, batch, seq, (_suffix_matrix(SB_TK).T,))
            o_s = _sb_sample_attention(page_table, q_rows, k_new, v_new, kv_pages(cache_k),
                                       kv_pages(cache_v), 0, u)
            state = kv_outputs(k, v)
        elif kind == "fox":
            lf_p = logf[:mp].reshape(batch, seq, N_HEADS)
            c_t = _cumsum_lanes(jnp.transpose(lf_p, (0, 2, 1)))
            c_t = c_t.reshape(batch, N_KV_HEADS, KV_GROUP, seq)
            c = jnp.transpose(c_t, (0, 1, 3, 2))
            op = _gqa_prompt_attention("fox", q, kb, vt, batch, seq, (c, c_t))
            lf_s = logf[mp:].reshape(n_seq, n_q, N_KV_HEADS, KV_GROUP)
            f_rows = jnp.transpose(lf_s, (0, 2, 3, 1))
            f_rows = jnp.broadcast_to(f_rows[:, :, None], (n_seq, N_KV_HEADS, n_q, KV_GROUP, n_q))
            f_rows = f_rows.reshape(n_seq, N_KV_HEADS, n_q * KV_GROUP, n_q)
            f_rows = jnp.pad(f_rows, ((0, 0), (0, 0), (0, 0), (0, LANES - n_q)))
            logf_t = jnp.transpose(cache_logf, (0, 1, 3, 2))
            o_s = _gqa_sample_attention("fox", page_table, q_rows, k_new, v_new, kv_pages(cache_k),
                                        kv_pages(cache_v), 0, (f_rows, logf_t))
            kp, vp, ks, vs = kv_outputs(k, v)
            state = (kp, vp, lf_p[None], ks, vs, logf[mp:].reshape(1, n_seq, n_q, N_HEADS))
        else:
            op = _gqa_prompt_attention("moba", q, kb, vt, batch, seq, (k, alibi_rows))
            o_s = _gqa_sample_attention("moba", page_table, q_rows, k_new, v_new, kv_pages(cache_k),
                                        kv_pages(cache_v), 0)
            state = kv_outputs(k, v)
        o = jnp.concatenate([op, _sample_rows_back(o_s, n_seq, n_q)], axis=0)
        return o, state

    def mla_layer(w_dq, q_norm, w_uq, w_dkv, kv_norm, w_ukv, cache_ckv, cache_kpe):
        q_rank = w_dq.shape[1]
        kv_rank = kv_norm.shape[0]
        ka, kb = _rotate_half_weights(w_dkv[:, kv_rank:])
        w_down = jnp.concatenate([w_dq, w_dkv[:, :kv_rank], ka, kb], axis=1).astype(BF16)
        cq, ckv, kpe = _fused_matmul(
            x, w_down, _mla_down_epilogue, [(q_rank, BF16), (kv_rank, F32), (LANES, F32)],
            row_ins=(cos_t, sin_t), const_ins=(q_norm.reshape(1, -1), kv_norm.reshape(1, -1)), name="mla_down")
        w_uq3 = w_uq.reshape(q_rank, N_HEADS, MLA_NOPE + MLA_ROPE)
        qa, qb = _rotate_half_weights(w_uq3[..., MLA_NOPE:])
        w_q = jnp.concatenate([w_uq3[..., :MLA_NOPE].reshape(q_rank, -1), qa.reshape(q_rank, -1),
                               qb.reshape(q_rank, -1)], axis=1).astype(BF16)
        q_nope, q_pe = _fused_matmul(cq, w_q, _mla_q_epilogue, [(nq_w, BF16), (nq_w, BF16)],
                                     row_ins=(cos_t, sin_t), name="mla_q")
        w_ukv3 = w_ukv.reshape(kv_rank, N_HEADS, MLA_NOPE + MLA_V)
        w_uk = w_ukv3[..., :MLA_NOPE]
        w_uv = w_ukv3[..., MLA_NOPE:]
        k_nope, v_t = _mla_kv_up(ckv[:mp], w_uk.reshape(kv_rank, -1).astype(BF16),
                                 jnp.transpose(w_uv, (1, 2, 0)).reshape(-1, kv_rank).astype(BF16))
        op = _mla_prompt_attention(q_nope, q_pe, k_nope, v_t, kpe, batch, seq)
        w_uk_t = jnp.transpose(w_uk, (1, 2, 0))
        w_uv_h = jnp.transpose(w_uv, (1, 0, 2))
        qn_s = jnp.transpose(q_nope[mp:].reshape(ms, N_HEADS, MLA_NOPE), (1, 0, 2))
        q_lat = _batched_matmul(qn_s, w_uk_t, BF16, "mla_q_lat")
        q_lat = jnp.transpose(q_lat.reshape(N_HEADS, n_seq, n_q, kv_rank), (1, 0, 2, 3))
        q_lat = q_lat.reshape(n_seq, N_HEADS * n_q, kv_rank)
        qr_s = jnp.transpose(q_pe[mp:].reshape(n_seq, n_q, N_HEADS, HEAD_DIM), (0, 2, 1, 3))
        qr_s = qr_s.reshape(n_seq, N_HEADS * n_q, HEAD_DIM)
        c_new = jnp.pad(ckv[mp:].reshape(n_seq, n_q, kv_rank), ((0, 0), (0, PAGE_SIZE - n_q), (0, 0)))
        r_new = jnp.pad(kpe[mp:, :MLA_ROPE].reshape(n_seq, n_q, MLA_ROPE), ((0, 0), (0, PAGE_SIZE - n_q), (0, 0)))
        o_lat = _mla_sample_attention(page_table, q_lat, qr_s, c_new, r_new, cache_ckv, cache_kpe, 0)
        o_lat = jnp.transpose(o_lat.reshape(n_seq, N_HEADS, n_q, kv_rank), (1, 0, 2, 3))
        o_s = _batched_matmul(o_lat.reshape(N_HEADS, ms, kv_rank), w_uv_h, BF16, "mla_o_up")
        o_s = jnp.transpose(o_s, (1, 0, 2)).reshape(ms, N_HEADS * MLA_V)
        o = jnp.concatenate([op, o_s], axis=0)
        state = (ckv[:mp].reshape(1, batch, seq, kv_rank), kpe[:mp, :MLA_ROPE].reshape(1, batch, seq, MLA_ROPE),
                 ckv[mp:].reshape(1, n_seq, n_q, kv_rank), kpe[mp:, :MLA_ROPE].reshape(1, n_seq, n_q, MLA_ROPE))
        return o, state

    states = {}
    for layer in range(DEPTH):
        kind = layer % 4
        if kind == 0:
            o, states["sb"] = gqa_layer("sb", sb_w_qkv[0], cache_sb_k, cache_sb_v)
            w_o = sb_w_o[0]
        elif kind == 1:
            o, states["mla"] = mla_layer(mla_w_dq[0], mla_q_norm[0], mla_w_uq[0], mla_w_dkv[0], mla_kv_norm[0],
                                         mla_w_ukv[0], cache_mla_ckv, cache_mla_kpe)
            w_o = mla_w_o[0]
        elif kind == 2:
            o, states["fox"] = gqa_layer("fox", fox_w_qkv[0], cache_fox_k, cache_fox_v, fox_w_f[0], fox_b_f[0],
                                         cache_fox_logf)
            w_o = fox_w_o[0]
        else:
            o, states["moba"] = gqa_layer("moba", moba_w_qkv[0], cache_moba_k, cache_moba_v)
            w_o = moba_w_o[0]
        x1, gates_t, sel_t, rank_t, counts = _oproj_norm_route(
            o, w_o.astype(BF16), x, ln_g[layer, 0].reshape(1, -1), ln_b[layer, 0].reshape(1, -1),
            rw_hi, rw_lo, rb)
        x = _sparse_moe_norm(x1, gates_t, sel_t, rank_t, counts, moe_w_gate[layer], moe_w_up[layer],
                             moe_w_down[layer], ln_g[layer, 1].reshape(1, -1), ln_b[layer, 1].reshape(1, -1))

    y_prompt = x[:mp].reshape(batch, seq, d_model)
    y_sample = x[mp:].reshape(n_seq, n_q, d_model)
    return (y_prompt, y_sample) + states["sb"] + states["mla"] + states["fox"] + states["moba"]
```

```python
import functools

import jax
import jax.numpy as jnp
from jax import lax
from jax.experimental import pallas as pl
from jax.experimental.pallas import tpu as pltpu

F32 = jnp.float32
BF16 = jnp.bfloat16

HEAD_DIM = 128
N_HEADS = 16
N_KV_HEADS = 2
KV_GROUP = N_HEADS // N_KV_HEADS
PAGE_SIZE = 128
MLA_NOPE = 128
MLA_ROPE = 64
MLA_V = 128
MLA_SCALE = (MLA_NOPE + MLA_ROPE) ** -0.5
ROPE_THETA = 10000.0
MOBA_BLOCK = 256
MOBA_TOPK = 3
N_EXPERTS = 16
EXPERTS_PER_GROUP = 4
N_GROUPS = N_EXPERTS // EXPERTS_PER_GROUP
DEPTH = 4
DN_ALPHA = (2 * DEPTH) ** 0.25
LN_EPS = 1e-5
RMS_EPS = 1e-6
ATT_SCALE = HEAD_DIM ** -0.5

LANES = 128
SUBLANES = 8
MASKED = -1e30
EXP_UNDERFLOW = -104.0
VMEM_LIMIT_BYTES = 56 * 1024 * 1024

NT_DIMS = (((1,), (1,)), ((), ()))


def _cparams(semantics):
    return pltpu.CompilerParams(dimension_semantics=semantics, vmem_limit_bytes=VMEM_LIMIT_BYTES)


def _dot(a, b):
    return jnp.dot(a, b, preferred_element_type=F32)


def _dot_nt(a, b):
    return lax.dot_general(a, b, NT_DIMS, preferred_element_type=F32)


def _split_bf16(x):
    hi = x.astype(BF16)
    lo = (x - hi.astype(F32)).astype(BF16)
    return hi, lo


def _dot_nt_precise(a, b):
    a_hi, a_lo = _split_bf16(a)
    b_hi, b_lo = _split_bf16(b)
    return _dot_nt(a_hi, b_hi) + _dot_nt(a_lo, b_hi) + _dot_nt(a_hi, b_lo)


def _log_sigmoid_pair(z):
    sp = jnp.log(1.0 + jnp.exp(-jnp.abs(z)))
    return jnp.minimum(z, 0.0) - sp, -jnp.maximum(z, 0.0) - sp


def _suffix_exclusive(x, u):
    hi, lo = _split_bf16(x)
    return _dot(hi, u) + _dot(lo, u)


def _layer_norm(y, g, b):
    mean = jnp.mean(y, axis=-1, keepdims=True)
    yc = y - mean
    var = jnp.mean(yc * yc, axis=-1, keepdims=True)
    return yc * lax.rsqrt(var + LN_EPS) * g + b


def _rms_norm(y, g):
    return y * lax.rsqrt(jnp.mean(y * y, axis=-1, keepdims=True) + RMS_EPS) * g


def _scores_by_page(q, k_pages):
    return jnp.concatenate([_dot_nt(q, k) for k in k_pages], axis=1)


def _values_by_page(p, v_pages):
    width = v_pages[0].shape[0]
    out = None
    for i, v in enumerate(v_pages):
        part = _dot(p[:, i * width:(i + 1) * width], v)
        out = part if out is None else out + part
    return out


def _softmax_update(s, valid, v_pages, m_ref, l_ref, acc_ref):
    m_old = m_ref[...]
    m_new = jnp.maximum(m_old, jnp.max(s, axis=1, keepdims=True))
    alpha = jnp.exp(m_old - m_new)
    p = jnp.exp(s - m_new)
    if valid is not None:
        p = jnp.where(valid, p, 0.0)
    l_ref[...] = alpha * l_ref[...] + jnp.sum(p, axis=1, keepdims=True)
    acc_ref[...] = alpha * acc_ref[...] + _values_by_page(p.astype(BF16), v_pages)
    m_ref[...] = m_new


def _fused_matmul_kernel(*refs, n_row, n_const, epilogue):
    x_ref, w_ref = refs[0], refs[1]
    row_refs = refs[2:2 + n_row]
    const_refs = refs[2 + n_row:2 + n_row + n_const]
    out_refs = refs[2 + n_row + n_const:]
    acc = _dot(x_ref[...].astype(BF16), w_ref[...])
    outs = epilogue(acc, [r[...] for r in row_refs], [c[...] for c in const_refs])
    for o_ref, val in zip(out_refs, outs):
        o_ref[...] = val.astype(o_ref.dtype)


def _fused_matmul(x, w, epilogue, outs, row_ins=(), const_ins=(), t_outs=(), tm=256, name="fused_matmul"):
    m, k = x.shape
    n = w.shape[1]
    assert m % tm == 0 and w.shape[0] == k
    in_specs = [pl.BlockSpec((tm, k), lambda i: (i, 0)),
                pl.BlockSpec((k, n), lambda i: (0, 0), pipeline_mode=pl.Buffered(1))]
    in_specs += [pl.BlockSpec((tm, a.shape[1]), lambda i: (i, 0)) for a in row_ins]
    in_specs += [pl.BlockSpec(c.shape, lambda i: (0, 0)) for c in const_ins]
    out_shape = [jax.ShapeDtypeStruct((m, width), dt) for width, dt in outs]
    out_specs = [pl.BlockSpec((tm, width), lambda i: (i, 0)) for width, _ in outs]
    out_shape += [jax.ShapeDtypeStruct((rows, m), dt) for rows, dt in t_outs]
    out_specs += [pl.BlockSpec((rows, tm), lambda i: (0, i)) for rows, _ in t_outs]
    return pl.pallas_call(
        functools.partial(_fused_matmul_kernel, n_row=len(row_ins), n_const=len(const_ins),
                          epilogue=epilogue),
        grid=(m // tm,), in_specs=in_specs, out_specs=out_specs, out_shape=out_shape,
        compiler_params=_cparams(("arbitrary",)), name=name,
    )(x, w, *row_ins, *const_ins)


def _bmm_kernel(x_ref, w_ref, o_ref):
    o_ref[...] = _dot(x_ref[...].astype(BF16), w_ref[...].astype(BF16)).astype(o_ref.dtype)


def _batched_matmul(x, w, out_dtype, name):
    h, m, k = x.shape
    n = w.shape[2]
    return pl.pallas_call(
        _bmm_kernel, grid=(h,),
        in_specs=[pl.BlockSpec((None, m, k), lambda i: (i, 0, 0)),
                  pl.BlockSpec((None, k, n), lambda i: (i, 0, 0))],
        out_specs=pl.BlockSpec((None, m, n), lambda i: (i, 0, 0)),
        out_shape=jax.ShapeDtypeStruct((h, m, n), out_dtype),
        compiler_params=_cparams(("arbitrary",)), name=name,
    )(x, w)


def _qkv_epilogue(acc, rows, consts):
    nq = N_HEADS * HEAD_DIM
    nk = N_KV_HEADS * HEAD_DIM
    k = acc[:, nq:nq + nk]
    v = acc[:, nq + nk:nq + 2 * nk]
    return acc[:, :nq], k, v, k, v.T


def _qkv_fox_epilogue(acc, rows, consts):
    nq = N_HEADS * HEAD_DIM
    nk = N_KV_HEADS * HEAD_DIM
    (b_f,) = consts
    logf, _ = _log_sigmoid_pair(acc[:, nq + 2 * nk:] + b_f)
    k = acc[:, nq:nq + nk]
    v = acc[:, nq + nk:nq + 2 * nk]
    return acc[:, :nq], k, v, logf, k, v.T


def _mla_down_epilogue(acc, rows, consts):
    cos, sin = rows
    q_norm, kv_norm = consts
    r = q_norm.shape[1]
    c = kv_norm.shape[1]
    cq = _rms_norm(acc[:, :r], q_norm)
    ckv = _rms_norm(acc[:, r:r + c], kv_norm)
    kpe = acc[:, r + c:r + c + LANES] * cos + acc[:, r + c + LANES:] * sin
    return cq, ckv, kpe


def _mla_q_epilogue(acc, rows, consts):
    cos, sin = rows
    n = N_HEADS * HEAD_DIM
    cos_t = jnp.concatenate([cos] * N_HEADS, axis=1)
    sin_t = jnp.concatenate([sin] * N_HEADS, axis=1)
    return acc[:, :n], acc[:, n:2 * n] * cos_t + acc[:, 2 * n:] * sin_t


def _route(x1, rw_hi, rw_lo, rb):
    x_hi, x_lo = _split_bf16(x1)
    logits = _dot_nt(rw_hi, x_hi) + _dot_nt(rw_hi, x_lo) + _dot_nt(rw_lo, x_hi)
    aff = 1.0 / (1.0 + jnp.exp(-logits))
    choice = aff + rb
    c = [choice[e:e + 1, :] for e in range(N_EXPERTS)]
    a = [aff[e:e + 1, :] for e in range(N_EXPERTS)]
    scores = []
    for g in range(N_GROUPS):
        cg = c[g * EXPERTS_PER_GROUP:(g + 1) * EXPERTS_PER_GROUP]
        best = None
        for i in range(EXPERTS_PER_GROUP):
            for j in range(i + 1, EXPERTS_PER_GROUP):
                pair = cg[i] + cg[j]
                best = pair if best is None else jnp.maximum(best, pair)
        scores.append(best)
    grp = jnp.zeros_like(scores[0], dtype=jnp.int32)
    best = scores[0]
    for g in range(1, N_GROUPS):
        better = scores[g] > best
        grp = jnp.where(better, g, grp)
        best = jnp.where(better, scores[g], best)
    w, sel = [], []
    for e in range(N_EXPERTS):
        g = e // EXPERTS_PER_GROUP
        rank = jnp.zeros_like(grp)
        for e2 in range(g * EXPERTS_PER_GROUP, (g + 1) * EXPERTS_PER_GROUP):
            if e2 == e:
                continue
            beats = (c[e2] >= c[e]) if e2 < e else (c[e2] > c[e])
            rank = rank + beats.astype(jnp.int32)
        selected = (grp == g) & (rank < 2)
        sel.append(selected.astype(F32))
        w.append(jnp.where(selected, a[e], 0.0))
    denom = w[0]
    for e in range(1, N_EXPERTS):
        denom = denom + w[e]
    return jnp.concatenate([we / denom for we in w], axis=0), jnp.concatenate(sel, axis=0)


def _oproj_norm_route_kernel(o_ref, w_ref, x_ref, g_ref, b_ref, rwh_ref, rwl_ref, rb_ref, before_ref,
                             x1_ref, gates_ref, sel_ref, rank_ref, counts_ref, carry_ref):
    @pl.when(pl.program_id(0) == 0)
    def _():
        carry_ref[...] = jnp.zeros_like(carry_ref)

    h = _dot(o_ref[...].astype(BF16), w_ref[...])
    x1 = _layer_norm(DN_ALPHA * x_ref[...] + h, g_ref[...], b_ref[...])
    x1_ref[...] = x1
    gates, sel = _route(x1, rwh_ref[...], rwl_ref[...], rb_ref[...])
    gates_ref[...] = gates
    sel_ref[...] = sel
    rank_ref[...] = _dot(sel.astype(BF16), before_ref[...]) + carry_ref[...]
    carry_ref[...] += jnp.sum(sel, axis=1, keepdims=True)
    counts_ref[...] = jnp.broadcast_to(carry_ref[...], counts_ref.shape)


def _oproj_norm_route(o, w_o, x, ln_g, ln_b, rw_hi, rw_lo, rb, tm=256):
    m, d = x.shape
    k = o.shape[1]
    before = _suffix_matrix(tm).T
    const = lambda shape: pl.BlockSpec(shape, lambda i: (0, 0))
    per_token = pl.BlockSpec((N_EXPERTS, tm), lambda i: (0, i))
    per_token_shape = jax.ShapeDtypeStruct((N_EXPERTS, m), F32)
    return pl.pallas_call(
        _oproj_norm_route_kernel, grid=(m // tm,),
        in_specs=[pl.BlockSpec((tm, k), lambda i: (i, 0)),
                  pl.BlockSpec((k, d), lambda i: (0, 0), pipeline_mode=pl.Buffered(1)),
                  pl.BlockSpec((tm, d), lambda i: (i, 0)),
                  const((1, d)), const((1, d)), const(rw_hi.shape), const(rw_lo.shape), const(rb.shape),
                  const(before.shape)],
        out_specs=[pl.BlockSpec((tm, d), lambda i: (i, 0)), per_token, per_token, per_token,
                   const((N_EXPERTS, LANES))],
        out_shape=[jax.ShapeDtypeStruct((m, d), F32), per_token_shape, per_token_shape, per_token_shape,
                   jax.ShapeDtypeStruct((N_EXPERTS, LANES), F32)],
        scratch_shapes=[pltpu.VMEM((N_EXPERTS, 1), F32)],
        compiler_params=_cparams(("arbitrary",)), name="oproj_norm_route",
    )(o, w_o, x, ln_g, ln_b, rw_hi, rw_lo, rb, before)


MOE_SLOT_TILE = 512
MOE_TOKEN_TILE = 256
N_CHOSEN = 2


def _row_copy(src_ref, src_row, dst_ref, dst_row, sem):
    return pltpu.make_async_copy(src_ref.at[pl.ds(src_row, 1), :], dst_ref.at[pl.ds(dst_row, 1), :], sem)


def _moe_dispatch_kernel(slot_ref, x_ref, zeros_ref, xs_ref, sem):
    tm = x_ref.shape[0]

    def issue(t, carry):
        for c in range(N_CHOSEN):
            _row_copy(x_ref, t, xs_ref, slot_ref[c, t], sem).start()
        return carry

    def drain(t, carry):
        for c in range(N_CHOSEN):
            _row_copy(x_ref, 0, xs_ref, 0, sem).wait()
        return carry

    lax.fori_loop(0, tm, issue, 0)
    lax.fori_loop(0, tm, drain, 0)


def _moe_dispatch(x1, slots, slot_buffer):
    m, d = x1.shape
    n_slots = slot_buffer.shape[0]
    tm = MOE_TOKEN_TILE
    return pl.pallas_call(
        _moe_dispatch_kernel, grid=(m // tm,),
        in_specs=[pl.BlockSpec((N_CHOSEN, tm), lambda i: (0, i), memory_space=pltpu.SMEM),
                  pl.BlockSpec((tm, d), lambda i: (i, 0)),
                  pl.BlockSpec(memory_space=pl.ANY)],
        out_specs=pl.BlockSpec(memory_space=pl.ANY),
        out_shape=jax.ShapeDtypeStruct((n_slots, d), F32),
        scratch_shapes=[pltpu.SemaphoreType.DMA(())],
        input_output_aliases={2: 0},
        compiler_params=_cparams(("arbitrary",)), name="moe_dispatch",
    )(slots, x1, slot_buffer)


def _moe_ffn_kernel(expert_ref, used_ref, xs_ref, wg_ref, wu_ref, wd_ref, ys_ref, wgb_ref, wub_ref, wdb_ref):
    i = pl.program_id(0)
    prev = expert_ref[jnp.maximum(i - 1, 0)]

    @pl.when(jnp.logical_or(i == 0, expert_ref[i] != prev))
    def _():
        wgb_ref[...] = wg_ref[...].astype(BF16)
        wub_ref[...] = wu_ref[...].astype(BF16)
        wdb_ref[...] = wd_ref[...].astype(BF16)

    @pl.when(i < used_ref[0])
    def _():
        xb = xs_ref[...].astype(BF16)
        hg = _dot(xb, wgb_ref[...])
        hu = _dot(xb, wub_ref[...])
        h = hg / (1.0 + jnp.exp(-hg)) * hu
        ys_ref[...] = _dot(h.astype(BF16), wdb_ref[...])

    @pl.when(i >= used_ref[0])
    def _():
        ys_ref[...] = jnp.zeros_like(ys_ref)


def _moe_ffn(xs, tile_expert, tiles_used, w_gate, w_up, w_down, layer):
    n_slots, d = xs.shape
    f = w_gate.shape[3]
    ts = MOE_SLOT_TILE
    row_map = lambda i, expert, used: (jnp.minimum(i, used[0] - 1), 0)
    w_map = lambda i, expert, used: (layer, expert[i], 0, 0)
    once = pl.Buffered(1)
    return pl.pallas_call(
        _moe_ffn_kernel,
        grid_spec=pltpu.PrefetchScalarGridSpec(
            num_scalar_prefetch=2, grid=(n_slots // ts,),
            in_specs=[pl.BlockSpec((ts, d), row_map),
                      pl.BlockSpec((None, None, d, f), w_map, pipeline_mode=once),
                      pl.BlockSpec((None, None, d, f), w_map, pipeline_mode=once),
                      pl.BlockSpec((None, None, f, d), w_map, pipeline_mode=once)],
            out_specs=pl.BlockSpec((ts, d), lambda i, expert, used: (i, 0)),
            scratch_shapes=[pltpu.VMEM((d, f), BF16), pltpu.VMEM((d, f), BF16), pltpu.VMEM((f, d), BF16)]),
        out_shape=jax.ShapeDtypeStruct((n_slots, d), F32),
        compiler_params=_cparams(("arbitrary",)), name="moe_ffn",
    )(tile_expert, tiles_used, xs, w_gate, w_up, w_down)


def _moe_combine_kernel(slot_ref, x_ref, gate_ref, g_ref, b_ref, ys_ref, o_ref, rows_ref, sem):
    tm = x_ref.shape[0]

    def issue(t, carry):
        for c in range(N_CHOSEN):
            _row_copy(ys_ref, slot_ref[c, t], rows_ref.at[c], t, sem).start()
        return carry

    def drain(t, carry):
        for c in range(N_CHOSEN):
            _row_copy(ys_ref, 0, rows_ref.at[c], 0, sem).wait()
        return carry

    lax.fori_loop(0, tm, issue, 0)
    lax.fori_loop(0, tm, drain, 0)
    gate = gate_ref[...]
    y = gate[:, 0:1] * rows_ref[0] + gate[:, 1:2] * rows_ref[1]
    o_ref[...] = _layer_norm(DN_ALPHA * x_ref[...] + y, g_ref[...], b_ref[...])


def _moe_combine_norm(x1, slots, gates2, ys, ln_g, ln_b):
    m, d = x1.shape
    tm = MOE_TOKEN_TILE
    return pl.pallas_call(
        _moe_combine_kernel, grid=(m // tm,),
        in_specs=[pl.BlockSpec((N_CHOSEN, tm), lambda i: (0, i), memory_space=pltpu.SMEM),
                  pl.BlockSpec((tm, d), lambda i: (i, 0)),
                  pl.BlockSpec((tm, N_CHOSEN), lambda i: (i, 0)),
                  pl.BlockSpec((1, d), lambda i: (0, 0)),
                  pl.BlockSpec((1, d), lambda i: (0, 0)),
                  pl.BlockSpec(memory_space=pl.ANY)],
        out_specs=pl.BlockSpec((tm, d), lambda i: (i, 0)),
        out_shape=jax.ShapeDtypeStruct((m, d), F32),
        scratch_shapes=[pltpu.VMEM((N_CHOSEN, tm, d), F32), pltpu.SemaphoreType.DMA(())],
        compiler_params=_cparams(("arbitrary",)), name="moe_combine_norm",
    )(slots, x1, gates2, ln_g, ln_b, ys)


def _moe_plan(gates_t, sel_t, rank_t, counts):
    m = gates_t.shape[1]
    ts = MOE_SLOT_TILE
    n_tiles = -(-(N_CHOSEN * m + N_EXPERTS * (ts - 1)) // ts)
    cnt = counts[:, 0].astype(jnp.int32)
    padded = (cnt + ts - 1) // ts * ts
    ends = jnp.cumsum(padded)
    starts = ends - padded
    tiles_used = ends[-1] // ts
    tile_ids = jnp.arange(n_tiles, dtype=jnp.int32)
    tile_expert = jnp.sum((tile_ids[:, None] * ts >= ends[None, :]).astype(jnp.int32), axis=1)
    last_expert = jnp.max(jnp.where(tile_ids < tiles_used, tile_expert, 0))
    tile_expert = jnp.where(tile_ids < tiles_used, tile_expert, last_expert)
    chosen = sel_t > 0.0
    slot = starts[:, None] + rank_t.astype(jnp.int32)
    slot_lo = jnp.min(jnp.where(chosen, slot, jnp.iinfo(jnp.int32).max), axis=0)
    slot_hi = jnp.max(jnp.where(chosen, slot, -1), axis=0)
    gate_lo = jnp.sum(jnp.where(chosen & (slot == slot_lo[None, :]), gates_t, 0.0), axis=0)
    gate_hi = jnp.sum(jnp.where(chosen & (slot == slot_hi[None, :]), gates_t, 0.0), axis=0)
    slots = jnp.stack([slot_lo, slot_hi], axis=0).astype(jnp.int32)
    gates2 = jnp.stack([gate_lo, gate_hi], axis=1)
    return slots, gates2, tile_expert.astype(jnp.int32), tiles_used.reshape(1).astype(jnp.int32), n_tiles * ts


def _sparse_moe_norm(x1, gates_t, sel_t, rank_t, counts, w_gate, w_up, w_down, layer, ln_g, ln_b, slot_buffer):
    slots, gates2, tile_expert, tiles_used, n_slots = _moe_plan(gates_t, sel_t, rank_t, counts)
    if slot_buffer is None:
        slot_buffer = jnp.zeros((n_slots, x1.shape[1]), F32)
    xs = _moe_dispatch(x1, slots, slot_buffer)
    ys = _moe_ffn(xs, tile_expert, tiles_used, w_gate, w_up, w_down, layer)
    return _moe_combine_norm(x1, slots, gates2, ys, ln_g, ln_b), xs


PROMPT_TQ = 128
PROMPT_TK = 256
SB_TK = 128
MLA_HEADS_PER_STEP = 8


def _key_positions(j, tk):
    return j * tk + lax.broadcasted_iota(jnp.int32, (tk, 1), 0)


def _query_positions(qi, tq):
    return qi * tq + lax.broadcasted_iota(jnp.int32, (1, tq), 1)


def _head_cols(h):
    return slice(h * HEAD_DIM, (h + 1) * HEAD_DIM)


HEADS_ISSUED_AHEAD = 8


def _issue_ahead(n, produce):
    queue = [produce(h) for h in range(min(HEADS_ISSUED_AHEAD, n))]
    for h in range(n):
        if h + HEADS_ISSUED_AHEAD < n:
            queue.append(produce(h + HEADS_ISSUED_AHEAD))
        yield h, queue.pop(0)


def _softmax_update_t(s, vt, h, m_ref, l_ref, acc_ref):
    m_old = m_ref[h:h + 1, :]
    m_new = jnp.maximum(m_old, jnp.max(s, axis=0, keepdims=True))
    alpha = jnp.exp(m_old - m_new)
    p = jnp.exp(s - m_new)
    l_ref[h:h + 1, :] = alpha * l_ref[h:h + 1, :] + jnp.sum(p, axis=0, keepdims=True)
    acc_ref[h] = alpha * acc_ref[h] + _dot(vt, p.astype(BF16))
    m_ref[h:h + 1, :] = m_new


def _init_softmax_state(m_ref, l_ref, acc_ref):
    m_ref[...] = jnp.full_like(m_ref, MASKED)
    l_ref[...] = jnp.zeros_like(l_ref)
    acc_ref[...] = jnp.zeros_like(acc_ref)


def _store_heads_t(o_ref, acc_ref, l_ref, n_heads):
    for h in range(n_heads):
        o = acc_ref[h] if l_ref is None else acc_ref[h] / l_ref[h:h + 1, :]
        o_ref[:, _head_cols(h)] = o.T.astype(o_ref.dtype)


def _sb_prompt_kernel(q_ref, k_ref, vt_ref, ut_ref, o_ref, qb_ref, run_ref, acc_ref, *, tq, tk):
    qi = pl.program_id(2)
    qb_ref[...] = q_ref[...].astype(BF16)
    run_ref[...] = jnp.zeros_like(run_ref)
    acc_ref[...] = jnp.zeros_like(acc_ref)
    tpos = _query_positions(qi, tq)
    jd = (qi * tq) // tk

    def chunk(j, diagonal):
        start = pl.multiple_of(j * tk, tk)
        k = k_ref[pl.ds(start, tk), :]
        vt = vt_ref[:, pl.ds(start, tk)]
        ut = ut_ref[...]
        if diagonal:
            past = _key_positions(j, tk) < tpos

        def gates(h):
            z = _dot_nt(k, qb_ref[:, _head_cols(h)]) * ATT_SCALE
            ls, lk = _log_sigmoid_pair(z)
            if diagonal:
                lk = jnp.where(past, lk, 0.0)
            hi, lo = _split_bf16(lk)
            later = _dot(ut, hi) + _dot(ut, lo)
            return ls, later, jnp.sum(lk, axis=0, keepdims=True)

        for h, (ls, later, total) in _issue_ahead(KV_GROUP, gates):
            run = run_ref[h:h + 1, :]
            w = jnp.exp(ls + (later + run))
            if diagonal:
                w = jnp.where(past, w, 0.0)
            acc_ref[h] += _dot(vt, w.astype(BF16))
            run_ref[h:h + 1, :] = run + total

    def live():
        return (jnp.max(run_ref[...]) > EXP_UNDERFLOW).astype(jnp.int32)

    chunk(jd, True)

    def cond(c):
        return jnp.logical_and(c[0] >= 0, c[1] > 0)

    def body(c):
        chunk(c[0], False)
        return c[0] - 1, live()

    lax.while_loop(cond, body, (jd - 1, live()))
    _store_heads_t(o_ref, acc_ref, None, KV_GROUP)


def _fox_prompt_kernel(q_ref, k_ref, vt_ref, c_ref, ct_ref, o_ref, qb_ref, m_ref, l_ref, acc_ref, *, tq, tk):
    qi = pl.program_id(2)
    qb_ref[...] = q_ref[...].astype(BF16)
    _init_softmax_state(m_ref, l_ref, acc_ref)
    tpos = _query_positions(qi, tq)
    q0 = pl.multiple_of(qi * tq, tq)
    jd = (qi * tq) // tk

    def chunk(j, diagonal):
        start = pl.multiple_of(j * tk, tk)
        k = k_ref[pl.ds(start, tk), :]
        vt = vt_ref[:, pl.ds(start, tk)]
        ck = c_ref[pl.ds(start, tk), :]
        if diagonal:
            valid = _key_positions(j, tk) <= tpos
        for h, qk in _issue_ahead(KV_GROUP, lambda h: _dot_nt(k, qb_ref[:, _head_cols(h)])):
            cq = ct_ref[h:h + 1, pl.ds(q0, tq)]
            s = qk * ATT_SCALE + cq - ck[:, h:h + 1]
            if diagonal:
                s = jnp.where(valid, s, MASKED)
            _softmax_update_t(s, vt, h, m_ref, l_ref, acc_ref)

    def body(j, carry):
        chunk(j, False)
        return carry

    lax.fori_loop(0, jd, body, 0)
    chunk(jd, True)
    _store_heads_t(o_ref, acc_ref, l_ref, KV_GROUP)


def _moba_select_t(gate, n_blocks_past, n_rows):
    row = lax.broadcasted_iota(jnp.int32, gate.shape, 0)
    is_past = row < n_blocks_past
    gate = jnp.where(is_past, gate, -jnp.inf)
    rank = jnp.zeros(gate.shape, jnp.int32)
    for b2 in range(n_rows):
        g2 = gate[b2:b2 + 1, :]
        beats = (g2 > gate) | ((g2 == gate) & (row > b2))
        rank = rank + beats.astype(jnp.int32)
    return (is_past & (rank < MOBA_TOPK) & (gate > -jnp.inf)).astype(F32)


def _moba_prompt_kernel(q_ref, k_ref, vt_ref, kf_ref, slope_ref, o_ref, qb_ref, kmean_ref, sel_ref,
                        m_ref, l_ref, acc_ref, *, tq, tk, n_blocks):
    qi = pl.program_id(2)

    @pl.when(qi == 0)
    def _():
        kmean_ref[...] = jnp.zeros_like(kmean_ref)
        for b in range(n_blocks):
            blk = kf_ref[b * MOBA_BLOCK:(b + 1) * MOBA_BLOCK, :]
            kmean_ref[b:b + 1, :] = jnp.sum(blk, axis=0, keepdims=True) * (1.0 / MOBA_BLOCK)

    qb_ref[...] = q_ref[...].astype(BF16)
    _init_softmax_state(m_ref, l_ref, acc_ref)
    tpos = _query_positions(qi, tq)
    own = (qi * tq) // MOBA_BLOCK
    kmean = kmean_ref[...]
    for h in range(KV_GROUP):
        sel_ref[h] = _moba_select_t(_dot_nt_precise(kmean, q_ref[:, _head_cols(h)]), own, n_blocks)

    def chunk(j, diagonal):
        start = pl.multiple_of(j * tk, tk)
        k = k_ref[pl.ds(start, tk), :]
        vt = vt_ref[:, pl.ds(start, tk)]
        kpos = _key_positions(j, tk)
        kdist = (kpos - tpos).astype(F32)
        if diagonal:
            valid = kpos <= tpos
        for h, qk in _issue_ahead(KV_GROUP, lambda h: _dot_nt(k, qb_ref[:, _head_cols(h)])):
            s = qk * ATT_SCALE + slope_ref[h:h + 1, :] * kdist
            if not diagonal:
                valid = sel_ref[h, pl.ds(j, 1), :] > 0.0
            s = jnp.where(valid, s, MASKED)
            _softmax_update_t(s, vt, h, m_ref, l_ref, acc_ref)

    def body(j, carry):
        chunk(j, False)
        return carry

    lax.fori_loop(0, own, body, 0)
    chunk(own, True)
    _store_heads_t(o_ref, acc_ref, l_ref, KV_GROUP)


def _gqa_prompt_attention(kind, q, kb, vt, batch, seq, extra=()):
    tq = PROMPT_TQ
    tk = {"sb": SB_TK, "fox": PROMPT_TK, "moba": MOBA_BLOCK}[kind]
    nq = seq // tq
    gw = KV_GROUP * HEAD_DIM
    q_spec = pl.BlockSpec((tq, gw), lambda b, h, i: (b * nq + i, h))
    k_spec = pl.BlockSpec((seq, HEAD_DIM), lambda b, h, i: (b, h))
    vt_spec = pl.BlockSpec((HEAD_DIM, seq), lambda b, h, i: (h, b))
    qb = pltpu.VMEM((tq, gw), BF16)
    row = pltpu.VMEM((KV_GROUP, tq), F32)
    acc = pltpu.VMEM((KV_GROUP, HEAD_DIM, tq), F32)
    if kind == "sb":
        (ut,) = extra
        kern = functools.partial(_sb_prompt_kernel, tq=tq, tk=tk)
        in_specs = [q_spec, k_spec, vt_spec, pl.BlockSpec(ut.shape, lambda b, h, i: (0, 0))]
        scratch = [qb, row, acc]
    elif kind == "fox":
        c, ct = extra
        kern = functools.partial(_fox_prompt_kernel, tq=tq, tk=tk)
        in_specs = [q_spec, k_spec, vt_spec,
                    pl.BlockSpec((None, None, seq, KV_GROUP), lambda b, h, i: (b, h, 0, 0)),
                    pl.BlockSpec((None, None, KV_GROUP, seq), lambda b, h, i: (b, h, 0, 0))]
        scratch = [qb, row, row, acc]
    else:
        kf, slopes = extra
        n_blocks = seq // MOBA_BLOCK
        nb_pad = -(-n_blocks // SUBLANES) * SUBLANES
        kern = functools.partial(_moba_prompt_kernel, tq=tq, tk=tk, n_blocks=n_blocks)
        in_specs = [q_spec, k_spec, vt_spec, k_spec,
                    pl.BlockSpec((None, KV_GROUP, tq), lambda b, h, i: (h, 0, 0))]
        scratch = [qb, pltpu.VMEM((nb_pad, HEAD_DIM), F32), pltpu.VMEM((KV_GROUP, nb_pad, tq), F32),
                   row, row, acc]
    return pl.pallas_call(
        kern, grid=(batch, N_KV_HEADS, nq), in_specs=in_specs, out_specs=q_spec,
        out_shape=jax.ShapeDtypeStruct((batch * seq, N_HEADS * HEAD_DIM), BF16),
        scratch_shapes=scratch,
        compiler_params=_cparams(("arbitrary", "arbitrary", "arbitrary")), name=kind + "_prompt",
    )(q, kb, vt, *extra)


def _mla_kv_up_kernel(c_ref, wk_ref, wvt_ref, kn_ref, vt_ref):
    c = c_ref[...].astype(BF16)
    kn_ref[...] = _dot(c, wk_ref[...]).astype(BF16)
    vt_ref[...] = _dot_nt(wvt_ref[...], c).astype(BF16)


def _mla_kv_up(ckv, w_uk, w_uv_t, tm=256):
    m, r = ckv.shape
    n = w_uk.shape[1]
    const = pl.Buffered(1)
    return pl.pallas_call(
        _mla_kv_up_kernel, grid=(m // tm,),
        in_specs=[pl.BlockSpec((tm, r), lambda i: (i, 0)),
                  pl.BlockSpec((r, n), lambda i: (0, 0), pipeline_mode=const),
                  pl.BlockSpec((n, r), lambda i: (0, 0), pipeline_mode=const)],
        out_specs=[pl.BlockSpec((tm, n), lambda i: (i, 0)), pl.BlockSpec((n, tm), lambda i: (0, i))],
        out_shape=[jax.ShapeDtypeStruct((m, n), BF16), jax.ShapeDtypeStruct((n, m), BF16)],
        compiler_params=_cparams(("arbitrary",)), name="mla_kv_up",
    )(ckv, w_uk, w_uv_t)


def _mla_prompt_kernel(qn_ref, qr_ref, kn_ref, vt_ref, kr_ref, o_ref, m_ref, l_ref, acc_ref, *, tq, tk, nh):
    qi = pl.program_id(2)
    _init_softmax_state(m_ref, l_ref, acc_ref)
    tpos = _query_positions(qi, tq)
    jd = (qi * tq) // tk

    def chunk(j, diagonal):
        start = pl.multiple_of(j * tk, tk)
        kr = kr_ref[pl.ds(start, tk), :].astype(BF16)
        if diagonal:
            valid = _key_positions(j, tk) <= tpos

        def scores(h):
            kn = kn_ref[pl.ds(start, tk), _head_cols(h)]
            return _dot_nt(kn, qn_ref[:, _head_cols(h)]) + _dot_nt(kr, qr_ref[:, _head_cols(h)])

        for h, qk in _issue_ahead(nh, scores):
            vt = vt_ref[_head_cols(h), pl.ds(start, tk)]
            s = qk * MLA_SCALE
            if diagonal:
                s = jnp.where(valid, s, MASKED)
            _softmax_update_t(s, vt, h, m_ref, l_ref, acc_ref)

    def body(j, carry):
        chunk(j, False)
        return carry

    lax.fori_loop(0, jd, body, 0)
    chunk(jd, True)
    _store_heads_t(o_ref, acc_ref, l_ref, nh)


def _mla_prompt_attention(q_nope, q_pe, k_nope, vt, k_pe, batch, seq):
    tq, tk, nh = PROMPT_TQ, PROMPT_TK, MLA_HEADS_PER_STEP
    nq = seq // tq
    gw = nh * HEAD_DIM
    q_spec = pl.BlockSpec((tq, gw), lambda b, h, i: (b * nq + i, h))
    row = pltpu.VMEM((nh, tq), F32)
    return pl.pallas_call(
        functools.partial(_mla_prompt_kernel, tq=tq, tk=tk, nh=nh), grid=(batch, N_HEADS // nh, nq),
        in_specs=[q_spec, q_spec,
                  pl.BlockSpec((seq, gw), lambda b, h, i: (b, h)),
                  pl.BlockSpec((gw, seq), lambda b, h, i: (h, b)),
                  pl.BlockSpec((seq, LANES), lambda b, h, i: (b, 0))],
        out_specs=q_spec,
        out_shape=jax.ShapeDtypeStruct((batch * seq, N_HEADS * MLA_V), BF16),
        scratch_shapes=[row, row, pltpu.VMEM((nh, MLA_V, tq), F32)],
        compiler_params=_cparams(("arbitrary", "arbitrary", "arbitrary")), name="mla_prompt",
    )(q_nope, q_pe, k_nope, vt, k_pe)


PAGES_PER_STEP = 16
MLA_PAGES_PER_STEP = 32
SB_HEAD_PAGES = 4
SB_TAIL_PAGES_PER_STEP = 12


def _page_specs(block, layer, n_pages, pages_per_step, first_slot):
    def make(p):
        def index_map(s, g, *prefetch):
            return (layer, prefetch[0][s * n_pages + first_slot(g) + p]) + (0,) * len(block)
        return pl.BlockSpec((None, None) + block, index_map)
    return [make(p) for p in range(pages_per_step)]


def _newest_first(n_groups, pages_per_step):
    return lambda g: (n_groups - 1 - g) * pages_per_step


def _kv_page_head(ref, kvh):
    return ref[pl.ds(kvh, PAGE_SIZE, stride=N_KV_HEADS), :].astype(BF16)


def _gather_kv(page_refs, kvh):
    return jnp.concatenate([_kv_page_head(r, kvh) for r in page_refs], axis=0)


def _sample_row_query(n_q):
    r = lax.broadcasted_iota(jnp.int32, (n_q * KV_GROUP, 1), 0)
    return r // KV_GROUP


def _sb_sample_chunk(kvh, q, k, v, past, u, run_ref, acc_ref):
    sub = u.shape[0]
    z = _dot_nt(q, k) * ATT_SCALE
    ls, lnk = _log_sigmoid_pair(z)
    for sb in range(k.shape[0] // sub - 1, -1, -1):
        cols = slice(sb * sub, (sb + 1) * sub)
        lk = lnk[:, cols] if past is None else jnp.where(past[:, cols], lnk[:, cols], 0.0)
        between = _suffix_exclusive(lk, u) + run_ref[kvh]
        w = jnp.exp(ls[:, cols] + between)
        if past is not None:
            w = jnp.where(past[:, cols], w, 0.0)
        acc_ref[kvh] += _dot(w.astype(BF16), v[cols, :])
        run_ref[kvh] += jnp.sum(lk, axis=1, keepdims=True)


def _sb_sample_head_kernel(pt_ref, q_ref, kn_ref, vn_ref, u_ref, *rest, n_q):
    k_pages = rest[:SB_HEAD_PAGES]
    v_pages = rest[SB_HEAD_PAGES:2 * SB_HEAD_PAGES]
    acc_out, run_out, run_ref, acc_ref = rest[2 * SB_HEAD_PAGES:]
    qidx = _sample_row_query(n_q)
    run_ref[...] = jnp.zeros_like(run_ref)
    acc_ref[...] = jnp.zeros_like(acc_ref)
    lane = lax.broadcasted_iota(jnp.int32, (1, PAGE_SIZE), 1)
    past = lane < qidx
    u_page = u_ref[:PAGE_SIZE, :PAGE_SIZE]
    for kvh in range(N_KV_HEADS):
        q = q_ref[kvh].astype(BF16)
        _sb_sample_chunk(kvh, q, _kv_page_head(kn_ref, kvh), _kv_page_head(vn_ref, kvh), past, u_page,
                         run_ref, acc_ref)
        _sb_sample_chunk(kvh, q, _gather_kv(k_pages, kvh), _gather_kv(v_pages, kvh), None, u_ref[...],
                         run_ref, acc_ref)
    acc_out[...] = acc_ref[...]
    run_out[...] = jnp.broadcast_to(run_ref[...], run_out.shape)


def _sb_sample_tail_kernel(pt_ref, done_ref, q_ref, acc_in, run_in, u_ref, *rest):
    n = SB_TAIL_PAGES_PER_STEP
    k_pages = rest[:n]
    v_pages = rest[n:2 * n]
    o_ref, run_ref, acc_ref = rest[2 * n:]
    s = pl.program_id(0)
    g = pl.program_id(1)

    @pl.when(g == 0)
    def _():
        run_ref[...] = run_in[...][:, :, :1]
        acc_ref[...] = acc_in[...]

    @pl.when(jnp.logical_and(done_ref[s] == 0, jnp.max(run_ref[...]) > EXP_UNDERFLOW))
    def _():
        for kvh in range(N_KV_HEADS):
            q = q_ref[kvh].astype(BF16)
            _sb_sample_chunk(kvh, q, _gather_kv(k_pages, kvh), _gather_kv(v_pages, kvh), None, u_ref[...],
                             run_ref, acc_ref)

    @pl.when(g == pl.num_programs(1) - 1)
    def _():
        o_ref[...] = acc_ref[...].astype(o_ref.dtype)


def _alibi_slopes_rows_sample(kvh, n_q):
    r = lax.broadcasted_iota(jnp.int32, (n_q * KV_GROUP, 1), 0)
    h = kvh * KV_GROUP + r % KV_GROUP
    return jnp.exp2(-8.0 * (h + 1).astype(F32) / N_HEADS)


def _softmax_sample_kernel(pt_ref, q_ref, kn_ref, vn_ref, *rest, n_q, kind, n_groups):
    if kind == "fox":
        fnew_ref = rest[0]
        f_pages = rest[1:1 + PAGES_PER_STEP]
        rest = rest[1 + PAGES_PER_STEP:]
    else:
        sel_ref = rest[0]
        rest = rest[1:]
    k_pages = rest[:PAGES_PER_STEP]
    v_pages = rest[PAGES_PER_STEP:2 * PAGES_PER_STEP]
    o_ref, m_ref, l_ref, acc_ref = rest[2 * PAGES_PER_STEP:2 * PAGES_PER_STEP + 4]
    fcarry_ref = rest[2 * PAGES_PER_STEP + 4] if kind == "fox" else None
    g = pl.program_id(1)
    rows = n_q * KV_GROUP
    qidx = _sample_row_query(n_q)
    chunk_keys = PAGES_PER_STEP * PAGE_SIZE
    past_len = n_groups * chunk_keys
    first_slot = (n_groups - 1 - g) * PAGES_PER_STEP

    @pl.when(g == 0)
    def _():
        m_ref[...] = jnp.full_like(m_ref, MASKED)
        l_ref[...] = jnp.zeros_like(l_ref)
        acc_ref[...] = jnp.zeros_like(acc_ref)
        lane = lax.broadcasted_iota(jnp.int32, (1, PAGE_SIZE), 1)
        valid = lane <= qidx
        if kind == "fox":
            fcarry_ref[...] = jnp.zeros_like(fcarry_ref)
        for kvh in range(N_KV_HEADS):
            q = q_ref[kvh].astype(BF16)
            s = _dot_nt(q, _kv_page_head(kn_ref, kvh)) * ATT_SCALE
            if kind == "fox":
                f = fnew_ref[kvh]
                bias = jnp.zeros((rows, PAGE_SIZE), F32)
                for j in range(n_q):
                    between = (lane > j) & (lane <= qidx)
                    bj = jnp.sum(jnp.where(between, f, 0.0), axis=1, keepdims=True)
                    bias = jnp.where(lane == j, bj, bias)
                s = s + bias
            else:
                slope = _alibi_slopes_rows_sample(kvh, n_q)
                s = s - slope * (qidx - lane).astype(F32)
            s = jnp.where(valid, s, MASKED)
            _softmax_update(s, valid, [_kv_page_head(vn_ref, kvh)], m_ref.at[kvh], l_ref.at[kvh], acc_ref.at[kvh])

    if kind == "fox":
        f = jnp.concatenate([r[...] for r in f_pages], axis=1)
        lane_c = lax.broadcasted_iota(jnp.int32, f.shape, 1)
        incl = f
        d = 1
        while d < chunk_keys:
            shifted = pltpu.roll(incl, chunk_keys - d, axis=1)
            incl = incl + jnp.where(lane_c + d < chunk_keys, shifted, 0.0)
            d *= 2
        suffix = incl - f + fcarry_ref[...]
        lane = lax.broadcasted_iota(jnp.int32, (1, PAGE_SIZE), 1)

    qk = [_dot_nt(q_ref[kvh].astype(BF16), _gather_kv(k_pages, kvh)) for kvh in range(N_KV_HEADS)]
    for kvh in range(N_KV_HEADS):
        s = qk[kvh] * ATT_SCALE
        if kind == "fox":
            f_new = fnew_ref[kvh]
            upto_q = jnp.sum(jnp.where(lane <= qidx, f_new, 0.0), axis=1, keepdims=True)
            sfx = suffix[kvh * KV_GROUP:(kvh + 1) * KV_GROUP, :]
            s = s + jnp.concatenate([sfx] * n_q, axis=0) + upto_q
            valid = None
        else:
            slope = _alibi_slopes_rows_sample(kvh, n_q)
            kpos = first_slot * PAGE_SIZE + lax.broadcasted_iota(jnp.int32, (1, chunk_keys), 1)
            s = s - slope * (past_len + qidx - kpos).astype(F32)
            sel = sel_ref[kvh]
            sel_lane = lax.broadcasted_iota(jnp.int32, sel.shape, 1)
            pages_per_block = MOBA_BLOCK // PAGE_SIZE
            cols = []
            for bb in range(PAGES_PER_STEP // pages_per_block):
                blk = first_slot // pages_per_block + bb
                picked = jnp.sum(jnp.where(sel_lane == blk, sel, 0.0), axis=1, keepdims=True) > 0.0
                cols.append(jnp.broadcast_to(picked, (rows, MOBA_BLOCK)))
            valid = jnp.concatenate(cols, axis=1)
            s = jnp.where(valid, s, MASKED)
        _softmax_update(s, valid, [_gather_kv(v_pages, kvh)], m_ref.at[kvh], l_ref.at[kvh], acc_ref.at[kvh])

    if kind == "fox":
        fcarry_ref[...] += jnp.sum(f, axis=1, keepdims=True)

    @pl.when(g == pl.num_programs(1) - 1)
    def _():
        o_ref[...] = (acc_ref[...] / l_ref[...]).astype(o_ref.dtype)


def _moba_select(gate, n_blocks_past, n_cols):
    lane = lax.broadcasted_iota(jnp.int32, gate.shape, 1)
    is_past = lane < n_blocks_past
    gate = jnp.where(is_past, gate, -jnp.inf)
    rank = jnp.zeros(gate.shape, jnp.int32)
    for b2 in range(n_cols):
        g2 = gate[:, b2:b2 + 1]
        beats = (g2 > gate) | ((g2 == gate) & (lane > b2))
        rank = rank + beats.astype(jnp.int32)
    return (is_past & (rank < MOBA_TOPK) & (gate > -jnp.inf)).astype(F32)


def _moba_sample_select_kernel(pt_ref, q_ref, *rest, n_groups):
    k_pages = rest[:PAGES_PER_STEP]
    sel_ref, kmean_ref = rest[PAGES_PER_STEP:]
    g = pl.program_id(1)
    pages_per_block = MOBA_BLOCK // PAGE_SIZE
    blocks_per_step = PAGES_PER_STEP // pages_per_block

    @pl.when(g == 0)
    def _():
        kmean_ref[...] = jnp.zeros_like(kmean_ref)

    first_block = pl.multiple_of((n_groups - 1 - g) * blocks_per_step, blocks_per_step)
    for kvh in range(N_KV_HEADS):
        means = []
        for bb in range(blocks_per_step):
            tot = None
            for p in range(pages_per_block):
                page = k_pages[bb * pages_per_block + p][pl.ds(kvh, PAGE_SIZE, stride=N_KV_HEADS), :]
                part = jnp.sum(page, axis=0, keepdims=True)
                tot = part if tot is None else tot + part
            means.append(tot * (1.0 / MOBA_BLOCK))
        kmean_ref[kvh, pl.ds(first_block, blocks_per_step), :] = jnp.concatenate(means, axis=0)

    @pl.when(g == n_groups - 1)
    def _():
        n_blocks = n_groups * blocks_per_step
        for kvh in range(N_KV_HEADS):
            gate = _dot_nt_precise(q_ref[kvh], kmean_ref[kvh])
            sel_ref[kvh] = _moba_select(gate, n_blocks, n_blocks)


def _sample_call(kern, name, prefetch, grid, fixed_ins, fixed_specs, paged_ins, paged_specs,
                 out_shape, out_spec, scratch):
    ins = list(fixed_ins)
    specs = list(fixed_specs)
    for arr, sp in zip(paged_ins, paged_specs):
        ins += [arr] * len(sp)
        specs += sp
    return pl.pallas_call(
        kern,
        grid_spec=pltpu.PrefetchScalarGridSpec(
            num_scalar_prefetch=len(prefetch), grid=grid, in_specs=specs, out_specs=out_spec,
            scratch_shapes=scratch),
        out_shape=out_shape,
        compiler_params=_cparams(("arbitrary", "arbitrary")), name=name,
    )(*prefetch, *ins)


def _seq_spec(shape):
    nd = len(shape)
    return pl.BlockSpec((None,) + shape, lambda s, g, *prefetch: (s,) + (0,) * nd)


def _const_spec(shape):
    nd = len(shape)
    return pl.BlockSpec(shape, lambda s, g, *prefetch: (0,) * nd)


def _sb_sample_attention(page_table, q_rows, k_new_page, v_new_page, cache_k, cache_v, layer, u):
    n_seq, _, rows, _ = q_rows.shape
    n_q = rows // KV_GROUP
    n_pages = page_table.shape[1]
    n_tail = n_pages - SB_HEAD_PAGES
    assert n_tail % SB_TAIL_PAGES_PER_STEP == 0
    tail_groups = n_tail // SB_TAIL_PAGES_PER_STEP
    kv_block = (PAGE_SIZE * N_KV_HEADS, HEAD_DIM)
    q_spec = _seq_spec((N_KV_HEADS, rows, HEAD_DIM))
    new_spec = _seq_spec(kv_block)
    state_shape = jax.ShapeDtypeStruct((n_seq, N_KV_HEADS, rows, HEAD_DIM), F32)
    col = pltpu.VMEM((N_KV_HEADS, rows, 1), F32)
    acc = pltpu.VMEM((N_KV_HEADS, rows, HEAD_DIM), F32)
    head_specs = _page_specs(kv_block, layer, n_pages, SB_HEAD_PAGES, lambda g: n_tail)
    acc0, run0 = _sample_call(
        functools.partial(_sb_sample_head_kernel, n_q=n_q), "sb_sample_head", [page_table.reshape(-1)],
        (n_seq, 1), [q_rows, k_new_page, v_new_page, u], [q_spec, new_spec, new_spec, _const_spec(u.shape)],
        [cache_k, cache_v], [head_specs, head_specs], [state_shape, state_shape], [q_spec, q_spec], [col, acc])
    done = jnp.max(run0, axis=(1, 2, 3)) < EXP_UNDERFLOW
    pt_tail = jnp.where(done[:, None], page_table[0, 0], page_table).reshape(-1)
    tail_specs = _page_specs(kv_block, layer, n_pages, SB_TAIL_PAGES_PER_STEP,
                             _newest_first(tail_groups, SB_TAIL_PAGES_PER_STEP))
    return _sample_call(
        _sb_sample_tail_kernel, "sb_sample_tail", [pt_tail, done.astype(jnp.int32)], (n_seq, tail_groups),
        [q_rows, acc0, run0, u], [q_spec, q_spec, q_spec, _const_spec(u.shape)],
        [cache_k, cache_v], [tail_specs, tail_specs],
        jax.ShapeDtypeStruct((n_seq, N_KV_HEADS, rows, HEAD_DIM), BF16), q_spec, [col, acc])


def _gqa_sample_attention(kind, page_table, q_rows, k_new_page, v_new_page, cache_k, cache_v, layer,
                          extra=None):
    n_seq, _, rows, _ = q_rows.shape
    n_q = rows // KV_GROUP
    n_pages = page_table.shape[1]
    n_groups = n_pages // PAGES_PER_STEP
    grid = (n_seq, n_groups)
    prefetch = [page_table.reshape(-1)]
    first_slot = _newest_first(n_groups, PAGES_PER_STEP)
    kv_block = (PAGE_SIZE * N_KV_HEADS, HEAD_DIM)
    kv_specs = _page_specs(kv_block, layer, n_pages, PAGES_PER_STEP, first_slot)
    q_spec = _seq_spec((N_KV_HEADS, rows, HEAD_DIM))
    new_spec = _seq_spec(kv_block)
    out_shape = jax.ShapeDtypeStruct((n_seq, N_KV_HEADS, rows, HEAD_DIM), BF16)
    col = pltpu.VMEM((N_KV_HEADS, rows, 1), F32)
    acc = pltpu.VMEM((N_KV_HEADS, rows, HEAD_DIM), F32)
    kern = functools.partial(_softmax_sample_kernel, n_q=n_q, kind=kind, n_groups=n_groups)
    if kind == "fox":
        f_new_rows, cache_logf_t = extra
        f_specs = _page_specs((N_HEADS, PAGE_SIZE), layer, n_pages, PAGES_PER_STEP, first_slot)
        return _sample_call(
            kern, "fox_sample", prefetch, grid,
            [q_rows, k_new_page, v_new_page, f_new_rows],
            [q_spec, new_spec, new_spec, _seq_spec((N_KV_HEADS, rows, LANES))],
            [cache_logf_t, cache_k, cache_v], [f_specs, kv_specs, kv_specs], out_shape, q_spec,
            [col, col, acc, pltpu.VMEM((N_HEADS, 1), F32)])
    sel = _sample_call(
        functools.partial(_moba_sample_select_kernel, n_groups=n_groups), "moba_sample_select",
        prefetch, grid, [q_rows], [q_spec], [cache_k], [kv_specs],
        jax.ShapeDtypeStruct((n_seq, N_KV_HEADS, rows, LANES), F32),
        _seq_spec((N_KV_HEADS, rows, LANES)), [pltpu.VMEM((N_KV_HEADS, LANES, HEAD_DIM), F32)])
    return _sample_call(
        kern, "moba_sample", prefetch, grid,
        [q_rows, k_new_page, v_new_page, sel],
        [q_spec, new_spec, new_spec, _seq_spec((N_KV_HEADS, rows, LANES))],
        [cache_k, cache_v], [kv_specs, kv_specs], out_shape, q_spec, [col, col, acc])


def _mla_sample_kernel(pt_ref, ql_ref, qr_ref, cn_ref, rn_ref, *rest, n_q, pps):
    c_pages = rest[:pps]
    r_pages = rest[pps:2 * pps]
    o_ref, m_ref, l_ref, acc_ref = rest[2 * pps:]
    g = pl.program_id(1)
    ql = ql_ref[...]
    qr = qr_ref[...][:, :MLA_ROPE]
    r = lax.broadcasted_iota(jnp.int32, (ql.shape[0], 1), 0)
    qidx = r % n_q

    @pl.when(g == 0)
    def _():
        m_ref[...] = jnp.full_like(m_ref, MASKED)
        l_ref[...] = jnp.zeros_like(l_ref)
        acc_ref[...] = jnp.zeros_like(acc_ref)
        c = cn_ref[...].astype(BF16)
        s = (_dot_nt(ql, c) + _dot(qr, rn_ref[...].astype(BF16))) * MLA_SCALE
        lane = lax.broadcasted_iota(jnp.int32, (1, PAGE_SIZE), 1)
        valid = lane <= qidx
        s = jnp.where(valid, s, MASKED)
        _softmax_update(s, valid, [c], m_ref, l_ref, acc_ref)

    c = [p[...].astype(BF16) for p in c_pages]
    rope_scores = jnp.concatenate([_dot(qr, p[...].astype(BF16)) for p in r_pages], axis=1)
    s = (_scores_by_page(ql, c) + rope_scores) * MLA_SCALE
    _softmax_update(s, None, c, m_ref, l_ref, acc_ref)

    @pl.when(g == pl.num_programs(1) - 1)
    def _():
        o_ref[...] = (acc_ref[...] / l_ref[...]).astype(o_ref.dtype)


def _mla_sample_attention(page_table, q_lat, q_pe, c_new_page, r_new_page, cache_ckv, cache_kpe, layer):
    n_seq, rows, rank = q_lat.shape
    n_q = rows // N_HEADS
    n_pages = page_table.shape[1]
    pps = MLA_PAGES_PER_STEP if n_pages % MLA_PAGES_PER_STEP == 0 else PAGES_PER_STEP
    n_groups = n_pages // pps
    first_slot = _newest_first(n_groups, pps)
    c_block = (PAGE_SIZE, rank)
    r_block = (MLA_ROPE, PAGE_SIZE)
    return _sample_call(
        functools.partial(_mla_sample_kernel, n_q=n_q, pps=pps), "mla_sample", [page_table.reshape(-1)],
        (n_seq, n_groups), [q_lat, q_pe, c_new_page, r_new_page],
        [_seq_spec((rows, rank)), _seq_spec((rows, LANES)), _seq_spec(c_block), _seq_spec(r_block)],
        [cache_ckv, cache_kpe],
        [_page_specs(c_block, layer, n_pages, pps, first_slot),
         _page_specs(r_block, layer, n_pages, pps, first_slot)],
        jax.ShapeDtypeStruct((n_seq, rows, rank), BF16), _seq_spec((rows, rank)),
        [pltpu.VMEM((rows, 1), F32), pltpu.VMEM((rows, 1), F32), pltpu.VMEM((rows, rank), F32)])


def _cumsum_lanes_kernel(x_ref, o_ref):
    x = x_ref[...]
    n = x.shape[1]
    lane = lax.broadcasted_iota(jnp.int32, x.shape, 1)
    d = 1
    while d < n:
        x = x + jnp.where(lane >= d, pltpu.roll(x, d, axis=1), 0.0)
        d *= 2
    o_ref[...] = x


def _cumsum_lanes(x):
    b, r, t = x.shape
    return pl.pallas_call(
        _cumsum_lanes_kernel, grid=(b,),
        in_specs=[pl.BlockSpec((None, r, t), lambda i: (i, 0, 0))],
        out_specs=pl.BlockSpec((None, r, t), lambda i: (i, 0, 0)),
        out_shape=jax.ShapeDtypeStruct(x.shape, F32),
        compiler_params=_cparams(("arbitrary",)), name="cumsum_lanes",
    )(x)


def _sample_rows(x, n_seq, n_q):
    x = x.reshape(n_seq, n_q, N_KV_HEADS, KV_GROUP, HEAD_DIM)
    return jnp.transpose(x, (0, 2, 1, 3, 4)).reshape(n_seq, N_KV_HEADS, n_q * KV_GROUP, HEAD_DIM)


def _sample_rows_back(o, n_seq, n_q):
    o = o.reshape(n_seq, N_KV_HEADS, n_q, KV_GROUP, HEAD_DIM)
    return jnp.transpose(o, (0, 2, 1, 3, 4)).reshape(n_seq * n_q, N_HEADS * HEAD_DIM)


def _new_page(rows, n_seq, n_q):
    w = rows.shape[1]
    per_tok = w // HEAD_DIM
    page = rows.reshape(n_seq, n_q * per_tok, HEAD_DIM)
    return jnp.pad(page, ((0, 0), (0, (PAGE_SIZE - n_q) * per_tok), (0, 0)))


def _suffix_matrix(n):
    j = lax.broadcasted_iota(jnp.int32, (n, n), 0)
    s = lax.broadcasted_iota(jnp.int32, (n, n), 1)
    return (j > s).astype(BF16)


def _rope_tables(pos):
    half = MLA_ROPE // 2
    inv = ROPE_THETA ** (-jnp.arange(half, dtype=F32) / half)
    ang = pos.astype(F32)[:, None] * inv[None, :]
    zeros = jnp.zeros((pos.shape[0], LANES - MLA_ROPE), F32)
    cos, sin = jnp.cos(ang), jnp.sin(ang)
    return jnp.concatenate([cos, cos, zeros], 1), jnp.concatenate([sin, sin, zeros], 1)


def _rotate_half_weights(w):
    half = MLA_ROPE // 2
    x1, x2 = w[..., :half], w[..., half:]
    pad = jnp.zeros(w.shape[:-1] + (LANES - MLA_ROPE,), w.dtype)
    return jnp.concatenate([x1, x2, pad], -1), jnp.concatenate([-x2, x1, pad], -1)


def kernel(x_prompt, x_sample, cache_sb_k, cache_sb_v, cache_mla_ckv, cache_mla_kpe, cache_fox_k, cache_fox_v, cache_fox_logf, cache_moba_k, cache_moba_v, page_table, ln_g, ln_b, sb_w_qkv, sb_w_o, mla_w_dq, mla_q_norm, mla_w_uq, mla_w_dkv, mla_kv_norm, mla_w_ukv, mla_w_o, fox_w_qkv, fox_w_f, fox_b_f, fox_w_o, moba_w_qkv, moba_w_o, router_w, router_b, moe_w_gate, moe_w_up, moe_w_down):
    batch, seq, d_model = x_prompt.shape
    n_seq, n_q, _ = x_sample.shape
    n_pages = page_table.shape[1]
    past_len = n_pages * PAGE_SIZE
    mp = batch * seq
    ms = n_seq * n_q
    m = mp + ms
    nq_w = N_HEADS * HEAD_DIM
    nk_w = N_KV_HEADS * HEAD_DIM

    x = jnp.concatenate([x_prompt.reshape(mp, d_model), x_sample.reshape(ms, d_model)], axis=0)
    u = _suffix_matrix(MOBA_BLOCK)
    rw_hi, rw_lo = _split_bf16(router_w.T)
    rb = router_b.reshape(N_EXPERTS, 1).astype(F32)
    pos = jnp.concatenate([jnp.tile(jnp.arange(seq, dtype=jnp.int32), batch),
                           jnp.tile(past_len + jnp.arange(n_q, dtype=jnp.int32), n_seq)])
    cos_t, sin_t = _rope_tables(pos)
    alibi = jnp.exp2(-8.0 * jnp.arange(1, N_HEADS + 1, dtype=F32) / N_HEADS)
    alibi_rows = jnp.broadcast_to(alibi.reshape(N_KV_HEADS, KV_GROUP, 1), (N_KV_HEADS, KV_GROUP, PROMPT_TQ))

    def kv_pages(cache):
        return cache.reshape(cache.shape[0], cache.shape[1], PAGE_SIZE * N_KV_HEADS, HEAD_DIM)

    def kv_outputs(k, v):
        kp = k[:mp].reshape(1, batch, seq, N_KV_HEADS, HEAD_DIM)
        vp = v[:mp].reshape(1, batch, seq, N_KV_HEADS, HEAD_DIM)
        ks = k[mp:].reshape(1, n_seq, n_q, N_KV_HEADS, HEAD_DIM)
        vs = v[mp:].reshape(1, n_seq, n_q, N_KV_HEADS, HEAD_DIM)
        return kp, vp, ks, vs

    def gqa_layer(kind, w_qkv, cache_k, cache_v, w_f=None, b_f=None, cache_logf=None):
        outs = [(nq_w, F32), (nk_w, F32), (nk_w, F32)]
        t_outs = [(nk_w, BF16)]
        if kind == "fox":
            w = jnp.concatenate([w_qkv, jnp.pad(w_f, ((0, 0), (0, LANES - N_HEADS)))], axis=1).astype(BF16)
            bf = jnp.pad(b_f.astype(F32), (0, LANES - N_HEADS)).reshape(1, LANES)
            q, k, v, logf, kb, vt = _fused_matmul(
                x, w, _qkv_fox_epilogue, outs + [(LANES, F32), (nk_w, BF16)], const_ins=(bf,),
                t_outs=t_outs, name="qkv_fox")
            logf = logf[:, :N_HEADS]
        else:
            q, k, v, kb, vt = _fused_matmul(x, w_qkv.astype(BF16), _qkv_epilogue, outs + [(nk_w, BF16)],
                                            t_outs=t_outs, name="qkv_" + kind)
        q_rows = _sample_rows(q[mp:], n_seq, n_q)
        k_new = _new_page(k[mp:], n_seq, n_q)
        v_new = _new_page(v[mp:], n_seq, n_q)
        if kind == "sb":
            op = _gqa_prompt_attention("sb", q, kb, vt, batch, seq, (_suffix_matrix(SB_TK).T,))
            o_s = _sb_sample_attention(page_table, q_rows, k_new, v_new, kv_pages(cache_k),
                                       kv_pages(cache_v), 0, u)
            state = kv_outputs(k, v)
        elif kind == "fox":
            lf_p = logf[:mp].reshape(batch, seq, N_HEADS)
            c_t = _cumsum_lanes(jnp.transpose(lf_p, (0, 2, 1)))
            c_t = c_t.reshape(batch, N_KV_HEADS, KV_GROUP, seq)
            c = jnp.transpose(c_t, (0, 1, 3, 2))
            op = _gqa_prompt_attention("fox", q, kb, vt, batch, seq, (c, c_t))
            lf_s = logf[mp:].reshape(n_seq, n_q, N_KV_HEADS, KV_GROUP)
            f_rows = jnp.transpose(lf_s, (0, 2, 3, 1))
            f_rows = jnp.broadcast_to(f_rows[:, :, None], (n_seq, N_KV_HEADS, n_q, KV_GROUP, n_q))
            f_rows = f_rows.reshape(n_seq, N_KV_HEADS, n_q * KV_GROUP, n_q)
            f_rows = jnp.pad(f_rows, ((0, 0), (0, 0), (0, 0), (0, LANES - n_q)))
            logf_t = jnp.transpose(cache_logf, (0, 1, 3, 2))
            o_s = _gqa_sample_attention("fox", page_table, q_rows, k_new, v_new, kv_pages(cache_k),
                                        kv_pages(cache_v), 0, (f_rows, logf_t))
            kp, vp, ks, vs = kv_outputs(k, v)
            state = (kp, vp, lf_p[None], ks, vs, logf[mp:].reshape(1, n_seq, n_q, N_HEADS))
        else:
            op = _gqa_prompt_attention("moba", q, kb, vt, batch, seq, (k, alibi_rows))
            o_s = _gqa_sample_attention("moba", page_table, q_rows, k_new, v_new, kv_pages(cache_k),
                                        kv_pages(cache_v), 0)
            state = kv_outputs(k, v)
        o = jnp.concatenate([op, _sample_rows_back(o_s, n_seq, n_q)], axis=0)
        return o, state

    def mla_layer(w_dq, q_norm, w_uq, w_dkv, kv_norm, w_ukv, cache_ckv, cache_kpe):
        q_rank = w_dq.shape[1]
        kv_rank = kv_norm.shape[0]
        ka, kb = _rotate_half_weights(w_dkv[:, kv_rank:])
        w_down = jnp.concatenate([w_dq, w_dkv[:, :kv_rank], ka, kb], axis=1).astype(BF16)
        cq, ckv, kpe = _fused_matmul(
            x, w_down, _mla_down_epilogue, [(q_rank, BF16), (kv_rank, F32), (LANES, F32)],
            row_ins=(cos_t, sin_t), const_ins=(q_norm.reshape(1, -1), kv_norm.reshape(1, -1)), name="mla_down")
        w_uq3 = w_uq.reshape(q_rank, N_HEADS, MLA_NOPE + MLA_ROPE)
        qa, qb = _rotate_half_weights(w_uq3[..., MLA_NOPE:])
        w_q = jnp.concatenate([w_uq3[..., :MLA_NOPE].reshape(q_rank, -1), qa.reshape(q_rank, -1),
                               qb.reshape(q_rank, -1)], axis=1).astype(BF16)
        q_nope, q_pe = _fused_matmul(cq, w_q, _mla_q_epilogue, [(nq_w, BF16), (nq_w, BF16)],
                                     row_ins=(cos_t, sin_t), name="mla_q")
        w_ukv3 = w_ukv.reshape(kv_rank, N_HEADS, MLA_NOPE + MLA_V)
        w_uk = w_ukv3[..., :MLA_NOPE]
        w_uv = w_ukv3[..., MLA_NOPE:]
        k_nope, v_t = _mla_kv_up(ckv[:mp], w_uk.reshape(kv_rank, -1).astype(BF16),
                                 jnp.transpose(w_uv, (1, 2, 0)).reshape(-1, kv_rank).astype(BF16))
        op = _mla_prompt_attention(q_nope, q_pe, k_nope, v_t, kpe, batch, seq)
        w_uk_t = jnp.transpose(w_uk, (1, 2, 0))
        w_uv_h = jnp.transpose(w_uv, (1, 0, 2))
        qn_s = jnp.transpose(q_nope[mp:].reshape(ms, N_HEADS, MLA_NOPE), (1, 0, 2))
        q_lat = _batched_matmul(qn_s, w_uk_t, BF16, "mla_q_lat")
        q_lat = jnp.transpose(q_lat.reshape(N_HEADS, n_seq, n_q, kv_rank), (1, 0, 2, 3))
        q_lat = q_lat.reshape(n_seq, N_HEADS * n_q, kv_rank)
        qr_s = jnp.transpose(q_pe[mp:].reshape(n_seq, n_q, N_HEADS, HEAD_DIM), (0, 2, 1, 3))
        qr_s = qr_s.reshape(n_seq, N_HEADS * n_q, HEAD_DIM)
        c_new = jnp.pad(ckv[mp:].reshape(n_seq, n_q, kv_rank), ((0, 0), (0, PAGE_SIZE - n_q), (0, 0)))
        r_new = jnp.pad(kpe[mp:, :MLA_ROPE].reshape(n_seq, n_q, MLA_ROPE), ((0, 0), (0, PAGE_SIZE - n_q), (0, 0)))
        o_lat = _mla_sample_attention(page_table, q_lat, qr_s, c_new, jnp.transpose(r_new, (0, 2, 1)), cache_ckv,
                                      jnp.transpose(cache_kpe, (0, 1, 3, 2)), 0)
        o_lat = jnp.transpose(o_lat.reshape(n_seq, N_HEADS, n_q, kv_rank), (1, 0, 2, 3))
        o_s = _batched_matmul(o_lat.reshape(N_HEADS, ms, kv_rank), w_uv_h, BF16, "mla_o_up")
        o_s = jnp.transpose(o_s, (1, 0, 2)).reshape(ms, N_HEADS * MLA_V)
        o = jnp.concatenate([op, o_s], axis=0)
        state = (ckv[:mp].reshape(1, batch, seq, kv_rank), kpe[:mp, :MLA_ROPE].reshape(1, batch, seq, MLA_ROPE),
                 ckv[mp:].reshape(1, n_seq, n_q, kv_rank), kpe[mp:, :MLA_ROPE].reshape(1, n_seq, n_q, MLA_ROPE))
        return o, state

    states = {}
    slot_buffer = None
    for layer in range(DEPTH):
        kind = layer % 4
        if kind == 0:
            o, states["sb"] = gqa_layer("sb", sb_w_qkv[0], cache_sb_k, cache_sb_v)
            w_o = sb_w_o[0]
        elif kind == 1:
            o, states["mla"] = mla_layer(mla_w_dq[0], mla_q_norm[0], mla_w_uq[0], mla_w_dkv[0], mla_kv_norm[0],
                                         mla_w_ukv[0], cache_mla_ckv, cache_mla_kpe)
            w_o = mla_w_o[0]
        elif kind == 2:
            o, states["fox"] = gqa_layer("fox", fox_w_qkv[0], cache_fox_k, cache_fox_v, fox_w_f[0], fox_b_f[0],
                                         cache_fox_logf)
            w_o = fox_w_o[0]
        else:
            o, states["moba"] = gqa_layer("moba", moba_w_qkv[0], cache_moba_k, cache_moba_v)
            w_o = moba_w_o[0]
        x1, gates_t, sel_t, rank_t, counts = _oproj_norm_route(
            o, w_o.astype(BF16), x, ln_g[layer, 0].reshape(1, -1), ln_b[layer, 0].reshape(1, -1),
            rw_hi, rw_lo, rb)
        x, slot_buffer = _sparse_moe_norm(x1, gates_t, sel_t, rank_t, counts, moe_w_gate, moe_w_up, moe_w_down,
                                          layer, ln_g[layer, 1].reshape(1, -1), ln_b[layer, 1].reshape(1, -1),
                                          slot_buffer)

    y_prompt = x[:mp].reshape(batch, seq, d_model)
    y_sample = x[mp:].reshape(n_seq, n_q, d_model)
    return (y_prompt, y_sample) + states["sb"] + states["mla"] + states["fox"] + states["moba"]
```
